```python
import jax, jax.numpy as jnp
from jax import lax
import numpy as np

D_MODEL = 1024
BATCH = 8
SEQ = 2048
DEPTH = 2

MEM_LEN = 256
RET_HEADS = 4
RET_DK = 64
RET_DV = 128
RET_CHUNK = 256
MOBA_HEADS = 4
MOBA_DH = 64
MOBA_BLOCK = 256
MOBA_TOPK = 3
MOBA_QCHUNK = 32
MEM_HEADS = 4
MEM_DH = 64
N_GROUPS = 4
EXPERTS_PER_GROUP = 4
N_EXPERTS = N_GROUPS * EXPERTS_PER_GROUP
TOPK_IN_GROUP = 2
EXPERT_FF = 512
EPS = 1e-6
PAD_MULT = 256

RET_QK = RET_HEADS * RET_DK
RET_V = RET_HEADS * RET_DV
MOBA_W = MOBA_HEADS * MOBA_DH
MEM_W = MEM_HEADS * MEM_DH
D_MIX = RET_V + MOBA_W + MEM_W
IN_SPLITS = [RET_QK, RET_QK, RET_V, RET_V, MOBA_W, MOBA_W, MOBA_W, MEM_W]
D_IN = sum(IN_SPLITS)
SPLIT_POINTS = [int(v) for v in np.cumsum(IN_SPLITS)[:-1]]

kernel_name = "hymba_retention_moba_memory_hmoe"


def rms_norm(x, g):
    xf = x.astype(jnp.float32)
    y = xf * lax.rsqrt(jnp.mean(xf * xf, axis=-1, keepdims=True) + EPS)
    return (y * g.astype(jnp.float32)).astype(x.dtype)


def split_heads(t, n_heads):
    b, s, _ = t.shape
    return t.reshape(b, s, n_heads, -1).transpose(0, 2, 1, 3)


def merge_heads(t):
    b, h, s, d = t.shape
    return t.transpose(0, 2, 1, 3).reshape(b, s, h * d)


def pad_seq(t, s_pad):
    return jnp.pad(t, ((0, 0), (0, 0), (0, s_pad - t.shape[2]), (0, 0)))


def rotary(t, pos):
    half = t.shape[-1] // 2
    inv = 1.0 / (10000.0 ** jnp.linspace(0.0, 1.0, half, dtype=jnp.float32))
    ang = pos.astype(jnp.float32)[:, None] * inv[None, :]
    cos = jnp.cos(ang).astype(t.dtype)
    sin = jnp.sin(ang).astype(t.dtype)
    t1, t2 = t[..., :half], t[..., half:]
    return jnp.concatenate([t1 * cos - t2 * sin, t1 * sin + t2 * cos], axis=-1)


def retention(q, k, v):
    b, h, s, dk = q.shape
    dv = v.shape[-1]
    c = RET_CHUNK
    nc = s // c
    dt = q.dtype
    log_g = jnp.log1p(-(2.0 ** (-5.0 - jnp.arange(h, dtype=jnp.float32))))
    n = jnp.arange(c, dtype=jnp.float32)
    diff = n[:, None] - n[None, :]
    dmask = jnp.where(diff >= 0, jnp.exp(log_g[:, None, None] * jnp.maximum(diff, 0.0)), 0.0)
    q_decay = jnp.exp(log_g[:, None] * (n + 1.0))
    k_decay = jnp.exp(log_g[:, None] * (c - 1.0 - n))
    chunk_decay = jnp.exp(log_g * c).astype(dt)
    qc = q.reshape(b, h, nc, c, dk)
    kc = k.reshape(b, h, nc, c, dk)
    vc = v.reshape(b, h, nc, c, dv)
    scores = jnp.einsum('bhncd,bhnmd->bhncm', qc, kc) * dmask[None, :, None].astype(dt)
    o_intra = jnp.einsum('bhncm,bhnmv->bhncv', scores, vc)
    upd = jnp.einsum('bhnmd,bhnmv->bhndv', kc * k_decay[None, :, None, :, None].astype(dt), vc)

    def step(state, u):
        return state * chunk_decay[None, :, None, None] + u, state

    _, r_prev = lax.scan(step, jnp.zeros((b, h, dk, dv), dt), jnp.moveaxis(upd, 2, 0))
    r_prev = jnp.moveaxis(r_prev, 0, 2)
    o_cross = jnp.einsum('bhncd,bhndv->bhncv', qc * q_decay[None, :, None, :, None].astype(dt), r_prev)
    return (o_intra + o_cross).reshape(b, h, s, dv)


def moba_attention(q, k, v):
    b, h, s, d = q.shape
    nb = s // MOBA_BLOCK
    kk = min(MOBA_TOPK, nb)
    scale = d ** -0.5
    kb = k.reshape(b, h, nb, MOBA_BLOCK, d)
    vb = v.reshape(b, h, nb, MOBA_BLOCK, d)
    kmean = jnp.mean(kb.astype(jnp.float32), axis=3).astype(k.dtype)
    n_qc = s // MOBA_QCHUNK
    q_chunks = jnp.moveaxis(q.reshape(b, h, n_qc, MOBA_QCHUNK, d), 2, 0)
    bi = jnp.arange(b)[:, None, None, None]
    hi = jnp.arange(h)[None, :, None, None]

    def one_chunk(args):
        qc, ci = args
        start = ci * MOBA_QCHUNK
        blk = start // MOBA_BLOCK
        qpos = start + jnp.arange(MOBA_QCHUNK)
        gate = jnp.einsum('bhqd,bhnd->bhqn', qc, kmean).astype(jnp.float32)
        gate = jnp.where(jnp.arange(nb) < blk, gate, -jnp.inf)
        _, idx = lax.top_k(gate, kk)
        valid = idx < blk
        k_sel = kb[bi, hi, idx]
        v_sel = vb[bi, hi, idx]
        s_sel = jnp.einsum('bhqd,bhqjtd->bhqjt', qc, k_sel).astype(jnp.float32) * scale
        s_sel = jnp.where(valid[..., None], s_sel, -jnp.inf).reshape(b, h, MOBA_QCHUNK, kk * MOBA_BLOCK)
        k_own = lax.dynamic_index_in_dim(kb, blk, axis=2, keepdims=False)
        v_own = lax.dynamic_index_in_dim(vb, blk, axis=2, keepdims=False)
        s_own = jnp.einsum('bhqd,bhtd->bhqt', qc, k_own).astype(jnp.float32) * scale
        kpos = blk * MOBA_BLOCK + jnp.arange(MOBA_BLOCK)
        s_own = jnp.where(kpos[None, :] <= qpos[:, None], s_own, -jnp.inf)
        p = jax.nn.softmax(jnp.concatenate([s_sel, s_own], axis=-1), axis=-1).astype(qc.dtype)
        o = jnp.einsum('bhqs,bhqsd->bhqd', p[..., :kk * MOBA_BLOCK],
                       v_sel.reshape(b, h, MOBA_QCHUNK, kk * MOBA_BLOCK, d))
        return o + jnp.einsum('bhqt,bhtd->bhqd', p[..., kk * MOBA_BLOCK:], v_own)

    out = lax.map(one_chunk, (q_chunks, jnp.arange(n_qc)))
    return jnp.moveaxis(out, 0, 2).reshape(b, h, s, d)


def memory_attention(q, mk, mv):
    s = jnp.einsum('bhsd,bhmd->bhsm', q, mk).astype(jnp.float32) * (q.shape[-1] ** -0.5)
    p = jax.nn.softmax(s, axis=-1).astype(q.dtype)
    return jnp.einsum('bhsm,bhmd->bhsd', p, mv)


def hier_moe(x, w_grp, b_grp, w_exp, b_exp, w_gate, w_up, w_down):
    b, s, d = x.shape
    t = x.reshape(b * s, d)
    g_prob = jax.nn.softmax((t @ w_grp + b_grp).astype(jnp.float32), axis=-1)
    g_w, g_idx = lax.top_k(g_prob, 1)
    e_logits = (t @ w_exp + b_exp).astype(jnp.float32).reshape(b * s, N_GROUPS, EXPERTS_PER_GROUP)
    e_in = jnp.take_along_axis(e_logits, g_idx[:, :, None], axis=1)[:, 0]
    e_prob = jax.nn.softmax(e_in, axis=-1)
    e_w, e_idx = lax.top_k(e_prob, TOPK_IN_GROUP)
    e_w = e_w / jnp.sum(e_w, axis=-1, keepdims=True)
    weights = g_w * e_w
    expert_id = g_idx * EXPERTS_PER_GROUP + e_idx
    combine = jnp.sum(jax.nn.one_hot(expert_id, N_EXPERTS, dtype=jnp.float32) * weights[..., None], axis=1)
    combine = combine.astype(t.dtype)
    out = jnp.zeros_like(t)
    for e in range(N_EXPERTS):
        hid = jax.nn.silu(t @ w_gate[e]) * (t @ w_up[e])
        out = out + combine[:, e:e + 1] * (hid @ w_down[e])
    return out.reshape(b, s, d)


def setup_inputs(seed: int = 0) -> dict:
    key = jax.random.key(seed)
    ks = jax.random.split(key, 24)
    f32 = jnp.float32

    def nrm(k, shape, fan_in):
        return jax.random.normal(k, shape, f32) * (fan_in ** -0.5)

    def gain(k, shape):
        return 1.0 + 0.02 * jax.random.normal(k, shape, f32)

    return {
        "x": jax.random.normal(ks[0], (BATCH, SEQ, D_MODEL), f32),
        "mem": jax.random.normal(ks[1], (BATCH, MEM_LEN, D_MODEL), f32),
        "norm1_g": gain(ks[2], (DEPTH, D_MODEL)),
        "w_in": nrm(ks[3], (DEPTH, D_MODEL, D_IN), D_MODEL),
        "ret_norm_g": gain(ks[4], (DEPTH, RET_HEADS, RET_DV)),
        "moba_qn_g": gain(ks[5], (DEPTH, MOBA_DH)),
        "moba_kn_g": gain(ks[6], (DEPTH, MOBA_DH)),
        "mem_norm_g": gain(ks[7], (DEPTH, D_MODEL)),
        "w_mem_kv": nrm(ks[8], (DEPTH, D_MODEL, 2 * MEM_W), D_MODEL),
        "mem_qn_g": gain(ks[9], (DEPTH, MEM_DH)),
        "mem_kn_g": gain(ks[10], (DEPTH, MEM_DH)),
        "w_out": nrm(ks[11], (DEPTH, D_MIX, D_MODEL), D_MIX),
        "norm2_g": gain(ks[12], (DEPTH, D_MODEL)),
        "w_router_grp": nrm(ks[13], (DEPTH, D_MODEL, N_GROUPS), D_MODEL),
        "b_router_grp": 0.01 * jax.random.normal(ks[14], (DEPTH, N_GROUPS), f32),
        "w_router_exp": nrm(ks[15], (DEPTH, D_MODEL, N_EXPERTS), D_MODEL),
        "b_router_exp": 0.01 * jax.random.normal(ks[16], (DEPTH, N_EXPERTS), f32),
        "w_exp_gate": nrm(ks[17], (DEPTH, N_EXPERTS, D_MODEL, EXPERT_FF), D_MODEL),
        "w_exp_up": nrm(ks[18], (DEPTH, N_EXPERTS, D_MODEL, EXPERT_FF), D_MODEL),
        "w_exp_down": nrm(ks[19], (DEPTH, N_EXPERTS, EXPERT_FF, D_MODEL), EXPERT_FF),
    }


def reference(x, mem, norm1_g, w_in, ret_norm_g, moba_qn_g, moba_kn_g, mem_norm_g, w_mem_kv,
              mem_qn_g, mem_kn_g, w_out, norm2_g, w_router_grp, b_router_grp, w_router_exp,
              b_router_exp, w_exp_gate, w_exp_up, w_exp_down):
    b, s, _ = x.shape
    s_pad = -(-s // PAD_MULT) * PAD_MULT
    pos = jnp.arange(s)
    for l in range(DEPTH):
        h = rms_norm(x, norm1_g[l])
        proj = h @ w_in[l]
        r_q, r_k, r_v, r_g, m_q, m_k, m_v, c_q = jnp.split(proj, SPLIT_POINTS, axis=-1)

        rq = rotary(split_heads(r_q, RET_HEADS), pos)
        rk = rotary(split_heads(r_k, RET_HEADS), pos) * (RET_DK ** -0.5)
        rv = split_heads(r_v, RET_HEADS)
        ro = retention(pad_seq(rq, s_pad), pad_seq(rk, s_pad), pad_seq(rv, s_pad))[:, :, :s]
        ro = rms_norm(ro, ret_norm_g[l][:, None, :])
        y_ret = jax.nn.silu(r_g) * merge_heads(ro)

        mq = rms_norm(split_heads(m_q, MOBA_HEADS), moba_qn_g[l])
        mk = rms_norm(split_heads(m_k, MOBA_HEADS), moba_kn_g[l])
        mv = split_heads(m_v, MOBA_HEADS)
        mo = moba_attention(pad_seq(mq, s_pad), pad_seq(mk, s_pad), pad_seq(mv, s_pad))[:, :, :s]
        y_moba = merge_heads(mo)

        mem_kv = rms_norm(mem, mem_norm_g[l]) @ w_mem_kv[l]
        ck = rms_norm(split_heads(mem_kv[..., :MEM_W], MEM_HEADS), mem_kn_g[l])
        cv = split_heads(mem_kv[..., MEM_W:], MEM_HEADS)
        cq = rms_norm(split_heads(c_q, MEM_HEADS), mem_qn_g[l])
        y_mem = merge_heads(memory_attention(cq, ck, cv))

        x = x + jnp.concatenate([y_ret, y_moba, y_mem], axis=-1) @ w_out[l]

        x = x + hier_moe(rms_norm(x, norm2_g[l]), w_router_grp[l], b_router_grp[l],
                         w_router_exp[l], b_router_exp[l], w_exp_gate[l], w_exp_up[l], w_exp_down[l])
    return x
```

```python
import functools

import numpy as np
import jax
import jax.numpy as jnp
from jax import lax
from jax.experimental import pallas as pl
from jax.experimental.pallas import tpu as pltpu

F32 = jnp.float32
BF16 = jnp.bfloat16

D_MODEL = 1024
MEM_LEN = 256
HEADS = 4
RET_DK = 64
RET_DV = 128
CHUNK = 256
MOBA_TOPK = 3
DH = 64
N_GROUPS = 4
EXPERTS_PER_GROUP = 4
N_EXPERTS = N_GROUPS * EXPERTS_PER_GROUP
EXPERT_FF = 512
EPS = 1e-6
D_IN = 2560
QW = HEADS * DH

COL_RQ, COL_RK, COL_MQ, COL_MK, COL_MV, COL_CQ = 0, 1, 6, 7, 8, 9
COL_RV512, COL_RG512 = 1, 2

PAIRS = ((0, 1), (0, 2), (0, 3), (1, 2), (1, 3), (2, 3))
N_BUCKETS = N_GROUPS * len(PAIRS)
EXT = 128
ROW_W = D_MODEL + EXT

TM_PROJ = 512
TM_OUT = 512
TQ_MEM = 512
TM_MOE = 256
VMEM_LIMIT = 48 * 1024 * 1024
NEG_INF = float("-inf")


def _cparams(sem):
    return pltpu.CompilerParams(dimension_semantics=sem, vmem_limit_bytes=VMEM_LIMIT)


def _dot(a, b):
    return jnp.dot(a, b, preferred_element_type=F32)


def _dot_nt(a, b, precision=None):
    return lax.dot_general(a, b, (((1,), (1,)), ((), ())), precision=precision,
                           preferred_element_type=F32)


def _dot_tn(a, b):
    return lax.dot_general(a, b, (((0,), (0,)), ((), ())), preferred_element_type=F32)


def _head_mean_sq(x, bmat):
    x2 = x * x
    hi = x2.astype(BF16)
    lo = (x2 - hi.astype(F32)).astype(BF16)
    return _dot(hi, bmat) + _dot(lo, bmat)


def _lane_head_mask(width, head, group):
    lane = lax.broadcasted_iota(jnp.int32, (1, width), 1)
    return jnp.where((lane // group) == head, 1.0, 0.0).astype(F32)


def _norm_proj_kernel(x_ref, g_ref, w_ref, o_ref):
    x = x_ref[...]
    ms = jnp.mean(x * x, axis=-1, keepdims=True)
    h = (x * lax.rsqrt(ms + EPS) * g_ref[...]).astype(BF16)
    for c in range(D_IN // 512):
        o_ref[:, c * 512:(c + 1) * 512] = _dot(h, w_ref[:, c * 512:(c + 1) * 512]).astype(BF16)


def _norm_proj(x2d, g, w_bf16):
    n = x2d.shape[0]
    return pl.pallas_call(
        _norm_proj_kernel,
        grid=(n // TM_PROJ,),
        in_specs=[
            pl.BlockSpec((TM_PROJ, D_MODEL), lambda i: (i, 0)),
            pl.BlockSpec((1, D_MODEL), lambda i: (0, 0)),
            pl.BlockSpec((D_MODEL, D_IN), lambda i: (0, 0)),
        ],
        out_specs=pl.BlockSpec((TM_PROJ, D_IN), lambda i: (i, 0)),
        out_shape=jax.ShapeDtypeStruct((n, D_IN), BF16),
        compiler_params=_cparams(("parallel",)),
        name="norm_proj",
    )(x2d, g, w_bf16)


def _retention_kernel(chunk_decay, q_ref, k_ref, v_ref, g_ref, cos_ref, sin_ref, dmask_ref,
                      qdec_ref, kdec_ref, gn_ref, o_ref, state_ref):
    n = pl.program_id(1)

    @pl.when(n == 0)
    def _():
        state_ref[...] = jnp.zeros_like(state_ref)

    cos = cos_ref[...]
    sin = sin_ref[...]

    def rot(t):
        t1, t2 = t[:, :128], t[:, 128:]
        return jnp.concatenate([t1 * cos - t2 * sin, t2 * cos + t1 * sin], axis=1)

    qr = rot(q_ref[0].astype(F32))
    kr = rot(k_ref[0].astype(F32)) * (RET_DK ** -0.5)
    qb = qr.astype(BF16)
    qd = (qr * qdec_ref[...]).astype(BF16)
    kd = kr * kdec_ref[...]
    lane = lax.broadcasted_iota(jnp.int32, (1, QW), 1)
    for h in range(HEADS):
        hm = jnp.where(((lane % 128) // 32) == h, 1.0, 0.0).astype(F32)
        km = (kr * hm).astype(BF16)
        vh = v_ref[0, :, h * RET_DV:(h + 1) * RET_DV]
        s = _dot_nt(qb, km) * dmask_ref[h]
        o = _dot(s.astype(BF16), vh)
        o = o + _dot(qd, state_ref[h].astype(BF16))
        upd = _dot_tn((kd * hm).astype(BF16), vh)
        state_ref[h] = state_ref[h] * chunk_decay[h] + upd
        ms = jnp.mean(o * o, axis=-1, keepdims=True)
        ro = o * lax.rsqrt(ms + EPS) * gn_ref[h:h + 1, :]
        gate = g_ref[0, :, h * RET_DV:(h + 1) * RET_DV].astype(F32)
        o_ref[0, :, h * RET_DV:(h + 1) * RET_DV] = (gate * jax.nn.sigmoid(gate) * ro).astype(BF16)


def _retention(proj3, cos_t, sin_t, dmask, qdec, kdec, gn, chunk_decay):
    b, s, _ = proj3.shape
    nc = s // CHUNK
    return pl.pallas_call(
        functools.partial(_retention_kernel, chunk_decay),
        grid=(b, nc),
        in_specs=[
            pl.BlockSpec((1, CHUNK, QW), lambda bi, n: (bi, n, COL_RQ)),
            pl.BlockSpec((1, CHUNK, QW), lambda bi, n: (bi, n, COL_RK)),
            pl.BlockSpec((1, CHUNK, 512), lambda bi, n: (bi, n, COL_RV512)),
            pl.BlockSpec((1, CHUNK, 512), lambda bi, n: (bi, n, COL_RG512)),
            pl.BlockSpec((CHUNK, 128), lambda bi, n: (n, 0)),
            pl.BlockSpec((CHUNK, 128), lambda bi, n: (n, 0)),
            pl.BlockSpec((HEADS, CHUNK, CHUNK), lambda bi, n: (0, 0, 0)),
            pl.BlockSpec((CHUNK, QW), lambda bi, n: (0, 0)),
            pl.BlockSpec((CHUNK, QW), lambda bi, n: (0, 0)),
            pl.BlockSpec((HEADS, RET_DV), lambda bi, n: (0, 0)),
        ],
        out_specs=pl.BlockSpec((1, CHUNK, HEADS * RET_DV), lambda bi, n: (bi, n, 0)),
        out_shape=jax.ShapeDtypeStruct((b, s, HEADS * RET_DV), BF16),
        scratch_shapes=[pltpu.VMEM((HEADS, QW, RET_DV), F32)],
        compiler_params=_cparams(("parallel", "arbitrary")),
        name="retention",
    )(proj3, proj3, proj3, proj3, cos_t, sin_t, dmask, qdec, kdec, gn)


def _moba_kernel(q_ref, k_ref, v_ref, gq_ref, gk_ref, bmat_ref, o_ref,
                 khat_ref, kmean_ref, qm_ref, sel_ref, m_ref, l_ref, acc_ref):
    i = pl.program_id(1)
    nb = khat_ref.shape[0]
    bmat = bmat_ref[...]
    lane = lax.broadcasted_iota(jnp.int32, (1, QW), 1)
    hmasks = [jnp.where((lane // DH) == h, 1.0, 0.0).astype(F32) for h in range(HEADS)]

    @pl.when(i == 0)
    def _():
        for n in range(nb):
            kb = k_ref[0, n * CHUNK:(n + 1) * CHUNK, :].astype(F32)
            kh = kb * lax.rsqrt(_head_mean_sq(kb, bmat) + EPS) * gk_ref[...]
            khat_ref[n] = kh.astype(BF16)
            km = jnp.mean(kh, axis=0, keepdims=True)
            for h in range(HEADS):
                kmean_ref[h * nb + n:h * nb + n + 1, :] = km * hmasks[h]

    q = q_ref[0].astype(F32)
    qn = q * lax.rsqrt(_head_mean_sq(q, bmat) + EPS) * gq_ref[...]
    for h in range(HEADS):
        qm_ref[h] = (qn * (hmasks[h] * (DH ** -0.5))).astype(BF16)

    gate = _dot_nt(kmean_ref[...], qn, precision=lax.Precision.HIGHEST)
    n_iota = lax.broadcasted_iota(jnp.int32, (nb, CHUNK), 0)
    valid = n_iota < i
    sel_rows = []
    for h in range(HEADS):
        gm = jnp.where(valid, gate[h * nb:(h + 1) * nb], NEG_INF)
        cnt = jnp.zeros((nb, CHUNK), F32)
        for n in range(nb):
            gn = gm[n:n + 1, :]
            beats = jnp.where(gn > gm, 1.0, jnp.where(gn == gm, jnp.where(n < n_iota, 1.0, 0.0), 0.0))
            cnt = cnt + beats
        sel_rows.append(jnp.where(valid, jnp.where(cnt < MOBA_TOPK, 1.0, 0.0), 0.0))
    sel_t = jnp.concatenate(sel_rows + [jnp.zeros((128 - HEADS * nb, CHUNK), F32)], axis=0)
    sel_ref[...] = sel_t.T

    row = lax.broadcasted_iota(jnp.int32, (CHUNK, CHUNK), 0)
    col = lax.broadcasted_iota(jnp.int32, (CHUNK, CHUNK), 1)
    causal = col <= row
    k_own = khat_ref[i]
    v_own = v_ref[0, pl.ds(pl.multiple_of(i * CHUNK, CHUNK), CHUNK), :]
    for h in range(HEADS):
        s = jnp.where(causal, _dot_nt(qm_ref[h], k_own), NEG_INF)
        m = jnp.max(s, axis=-1, keepdims=True)
        p = jnp.exp(s - m)
        m_ref[h] = jnp.broadcast_to(m, (CHUNK, 128))
        l_ref[h] = jnp.broadcast_to(jnp.sum(p, axis=-1, keepdims=True), (CHUNK, 128))
        acc_ref[h] = _dot(p.astype(BF16), v_own)

    for j in range(nb - 1):
        @pl.when(j < i)
        def _(j=j):
            kj = khat_ref[j]
            vj = v_ref[0, j * CHUNK:(j + 1) * CHUNK, :]
            for h in range(HEADS):
                c = h * nb + j
                s = jnp.where(sel_ref[:, c:c + 1] > 0.5, _dot_nt(qm_ref[h], kj), NEG_INF)
                m_prev = m_ref[h][:, :1]
                m_new = jnp.maximum(m_prev, jnp.max(s, axis=-1, keepdims=True))
                alpha = jnp.exp(m_prev - m_new)
                p = jnp.exp(s - m_new)
                l_new = alpha * l_ref[h][:, :1] + jnp.sum(p, axis=-1, keepdims=True)
                m_ref[h] = jnp.broadcast_to(m_new, (CHUNK, 128))
                l_ref[h] = jnp.broadcast_to(l_new, (CHUNK, 128))
                acc_ref[h] = alpha * acc_ref[h] + _dot(p.astype(BF16), vj)

    out = jnp.zeros((CHUNK, QW), F32)
    for h in range(HEADS):
        out = out + hmasks[h] * (acc_ref[h] / l_ref[h][:, :1])
    o_ref[0] = out.astype(BF16)


def _moba(proj3, gq, gk, bmat):
    b, s, _ = proj3.shape
    nb = s // CHUNK
    return pl.pallas_call(
        _moba_kernel,
        grid=(b, nb),
        in_specs=[
            pl.BlockSpec((1, CHUNK, QW), lambda bi, i: (bi, i, COL_MQ)),
            pl.BlockSpec((1, s, QW), lambda bi, i: (bi, 0, COL_MK)),
            pl.BlockSpec((1, s, QW), lambda bi, i: (bi, 0, COL_MV)),
            pl.BlockSpec((1, QW), lambda bi, i: (0, 0)),
            pl.BlockSpec((1, QW), lambda bi, i: (0, 0)),
            pl.BlockSpec((QW, QW), lambda bi, i: (0, 0)),
        ],
        out_specs=pl.BlockSpec((1, CHUNK, QW), lambda bi, i: (bi, i, 0)),
        out_shape=jax.ShapeDtypeStruct((b, s, QW), BF16),
        scratch_shapes=[
            pltpu.VMEM((nb, CHUNK, QW), BF16),
            pltpu.VMEM((HEADS * nb, QW), F32),
            pltpu.VMEM((HEADS, CHUNK, QW), BF16),
            pltpu.VMEM((CHUNK, 128), F32),
            pltpu.VMEM((HEADS, CHUNK, 128), F32),
            pltpu.VMEM((HEADS, CHUNK, 128), F32),
            pltpu.VMEM((HEADS, CHUNK, QW), F32),
        ],
        compiler_params=_cparams(("parallel", "arbitrary")),
        name="moba",
    )(proj3, proj3, proj3, gq, gk, bmat)


def _mem_attn_kernel(q_ref, mem_ref, gm_ref, w_ref, gq_ref, gk_ref, bmat_ref, o_ref, ck_ref, cv_ref):
    i = pl.program_id(1)
    bmat = bmat_ref[...]

    @pl.when(i == 0)
    def _():
        mem = mem_ref[0]
        ms = jnp.mean(mem * mem, axis=-1, keepdims=True)
        mn = (mem * lax.rsqrt(ms + EPS) * gm_ref[...]).astype(BF16)
        kv = _dot(mn, w_ref[...])
        ck = kv[:, :QW]
        ck_ref[...] = (ck * lax.rsqrt(_head_mean_sq(ck, bmat) + EPS) * gk_ref[...]).astype(BF16)
        cv_ref[...] = kv[:, QW:].astype(BF16)

    q = q_ref[0].astype(F32)
    qn = q * lax.rsqrt(_head_mean_sq(q, bmat) + EPS) * gq_ref[...]
    lane = lax.broadcasted_iota(jnp.int32, (1, QW), 1)
    out = jnp.zeros(qn.shape, F32)
    for h in range(HEADS):
        hm = jnp.where((lane // DH) == h, 1.0, 0.0).astype(F32)
        s = _dot_nt((qn * (hm * (DH ** -0.5))).astype(BF16), ck_ref[...])
        p = jnp.exp(s - jnp.max(s, axis=-1, keepdims=True))
        l = jnp.sum(p, axis=-1, keepdims=True)
        out = out + hm * (_dot(p.astype(BF16), cv_ref[...]) / l)
    o_ref[0] = out.astype(BF16)


def _mem_attn(proj3, mem, gm, w_bf16, gq, gk, bmat):
    b, s, _ = proj3.shape
    return pl.pallas_call(
        _mem_attn_kernel,
        grid=(b, s // TQ_MEM),
        in_specs=[
            pl.BlockSpec((1, TQ_MEM, QW), lambda bi, i: (bi, i, COL_CQ)),
            pl.BlockSpec((1, MEM_LEN, D_MODEL), lambda bi, i: (bi, 0, 0)),
            pl.BlockSpec((1, D_MODEL), lambda bi, i: (0, 0)),
            pl.BlockSpec((D_MODEL, 2 * QW), lambda bi, i: (0, 0)),
            pl.BlockSpec((1, QW), lambda bi, i: (0, 0)),
            pl.BlockSpec((1, QW), lambda bi, i: (0, 0)),
            pl.BlockSpec((QW, QW), lambda bi, i: (0, 0)),
        ],
        out_specs=pl.BlockSpec((1, TQ_MEM, QW), lambda bi, i: (bi, i, 0)),
        out_shape=jax.ShapeDtypeStruct((b, s, QW), BF16),
        scratch_shapes=[pltpu.VMEM((MEM_LEN, QW), BF16), pltpu.VMEM((MEM_LEN, QW), BF16)],
        compiler_params=_cparams(("parallel", "arbitrary")),
        name="mem_attn",
    )(proj3, mem, gm, w_bf16, gq, gk, bmat)


ROUTER_ROWS = 32
EXP_ROW0 = 8


def _out_route_kernel(x_ref, yr_ref, ym_ref, yc_ref, wo_ref, g2_ref, wr_ref, br_ref, xe_ref, rt_ref):
    x1 = x_ref[...]
    x1 = x1 + _dot(yr_ref[...], wo_ref[0:512, :])
    x1 = x1 + _dot(ym_ref[...], wo_ref[512:768, :])
    x1 = x1 + _dot(yc_ref[...], wo_ref[768:1024, :])
    ms = jnp.mean(x1 * x1, axis=-1, keepdims=True)
    xn = x1 * lax.rsqrt(ms + EPS) * g2_ref[...]
    lt = _dot_nt(wr_ref[...], xn, precision=lax.Precision.HIGHEST) + br_ref[...]
    tm = lt.shape[1]
    gl = lt[0:N_GROUPS]
    gmax = jnp.max(gl, axis=0, keepdims=True)
    g_iota = lax.broadcasted_iota(jnp.int32, (N_GROUPS, tm), 0)
    gidx = jnp.min(jnp.where(gl == gmax, g_iota, N_GROUPS), axis=0, keepdims=True)
    g_w = 1.0 / jnp.sum(jnp.exp(gl - gmax), axis=0, keepdims=True)
    e_in = jnp.zeros((EXPERTS_PER_GROUP, tm), F32)
    for g in range(N_GROUPS):
        r0 = EXP_ROW0 + g * EXPERTS_PER_GROUP
        e_in = jnp.where(gidx == g, lt[r0:r0 + EXPERTS_PER_GROUP], e_in)
    e_iota = lax.broadcasted_iota(jnp.int32, (EXPERTS_PER_GROUP, tm), 0)
    cnt = jnp.zeros((EXPERTS_PER_GROUP, tm), F32)
    for n in range(EXPERTS_PER_GROUP):
        en = e_in[n:n + 1, :]
        cnt = cnt + jnp.where(en > e_in, 1.0, jnp.where(en == e_in, jnp.where(n < e_iota, 1.0, 0.0), 0.0))
    sel = cnt < 2.0
    a_idx = jnp.min(jnp.where(sel, e_iota, EXPERTS_PER_GROUP), axis=0, keepdims=True)
    b_idx = jnp.max(jnp.where(sel, e_iota, -1), axis=0, keepdims=True)
    la = jnp.sum(jnp.where(e_iota == a_idx, e_in, 0.0), axis=0, keepdims=True)
    lb = jnp.sum(jnp.where(e_iota == b_idx, e_in, 0.0), axis=0, keepdims=True)
    mx = jnp.maximum(la, lb)
    ea = jnp.exp(la - mx)
    eb = jnp.exp(lb - mx)
    wa = g_w * (ea / (ea + eb))
    wb = g_w * (eb / (ea + eb))
    pair_off = jnp.where(a_idx == 0, 0, jnp.where(a_idx == 1, 3, 5))
    bucket = (gidx * len(PAIRS) + pair_off + (b_idx - a_idx - 1)).astype(F32)
    rows = jnp.concatenate([bucket, wa, wb, jnp.zeros((5, tm), F32)], axis=0)
    rt_ref[...] = rows
    ext = jnp.concatenate([rows, jnp.zeros((EXT - 8, tm), F32)], axis=0).T
    xe_ref[:, :D_MODEL] = x1
    xe_ref[:, D_MODEL:] = ext


def _out_route(x2d, y_ret, y_moba, y_mem, wo_bf16, g2, wr_t, br):
    n = x2d.shape[0]
    return pl.pallas_call(
        _out_route_kernel,
        grid=(n // TM_OUT,),
        in_specs=[
            pl.BlockSpec((TM_OUT, D_MODEL), lambda i: (i, 0)),
            pl.BlockSpec((TM_OUT, 512), lambda i: (i, 0)),
            pl.BlockSpec((TM_OUT, QW), lambda i: (i, 0)),
            pl.BlockSpec((TM_OUT, QW), lambda i: (i, 0)),
            pl.BlockSpec((D_MODEL, D_MODEL), lambda i: (0, 0)),
            pl.BlockSpec((1, D_MODEL), lambda i: (0, 0)),
            pl.BlockSpec((ROUTER_ROWS, D_MODEL), lambda i: (0, 0)),
            pl.BlockSpec((ROUTER_ROWS, 1), lambda i: (0, 0)),
        ],
        out_specs=[
            pl.BlockSpec((TM_OUT, ROW_W), lambda i: (i, 0)),
            pl.BlockSpec((8, TM_OUT), lambda i: (0, i)),
        ],
        out_shape=[jax.ShapeDtypeStruct((n, ROW_W), F32), jax.ShapeDtypeStruct((8, n), F32)],
        compiler_params=_cparams(("parallel",)),
        name="out_route",
    )(x2d, y_ret, y_moba, y_mem, wo_bf16, g2, wr_t, br)


def _moe_kernel(ea_ref, eb_ref, nv_ref, tok_ref, x_hbm, g2_ref, wga_ref, wua_ref, wda_ref,
                wgb_ref, wub_ref, wdb_ref, o_hbm, xbuf, obuf, gsem, ssem):
    t = pl.program_id(0)
    n_tiles = pl.num_programs(0)
    slot = t % 2

    def scatter_copy(s, r, tok):
        return pltpu.make_async_copy(obuf.at[s, pl.ds(r, 1)], o_hbm.at[pl.ds(tok, 1)], ssem.at[s])

    def wait_scatter(s, count):
        full = pl.multiple_of((count >> 3) << 3, 8)

        @pl.when(full > 0)
        def _():
            pltpu.make_async_copy(obuf.at[s, pl.ds(0, full)], o_hbm.at[pl.ds(0, full)], ssem.at[s]).wait()

        for p in (4, 2, 1):
            @pl.when((count & p) != 0)
            def _():
                for _i in range(p):
                    scatter_copy(s, 0, 0).wait()

    @pl.when(t >= 2)
    def _():
        wait_scatter(slot, nv_ref[jnp.maximum(t - 2, 0)])

    nv = nv_ref[t]

    @pl.when(nv > 0)
    def _():
        def issue(r, carry):
            tok = tok_ref[0, 0, r]
            pltpu.make_async_copy(x_hbm.at[pl.ds(tok, 1)], xbuf.at[pl.ds(r, 1)], gsem).start()
            return carry

        lax.fori_loop(0, TM_MOE, issue, 0)
        pltpu.make_async_copy(x_hbm.at[pl.ds(0, TM_MOE)], xbuf, gsem).wait()

        x1 = xbuf[:, :D_MODEL]
        wa = xbuf[:, D_MODEL + 1:D_MODEL + 2]
        wb = xbuf[:, D_MODEL + 2:D_MODEL + 3]
        ms = jnp.mean(x1 * x1, axis=-1, keepdims=True)
        xn = (x1 * lax.rsqrt(ms + EPS) * g2_ref[...]).astype(BF16)

        def expert(wg_ref, wu_ref, wd_ref):
            g = _dot(xn, wg_ref[0])
            u = _dot(xn, wu_ref[0])
            hid = (g * jax.nn.sigmoid(g) * u).astype(BF16)
            return _dot(hid, wd_ref[0])

        y = x1 + wa * expert(wga_ref, wua_ref, wda_ref)
        y = y + wb * expert(wgb_ref, wub_ref, wdb_ref)
        obuf[slot] = y

        def emit(r, carry):
            scatter_copy(slot, r, tok_ref[0, 0, r]).start()
            return carry

        lax.fori_loop(0, nv, emit, 0)

    @pl.when(t == n_tiles - 1)
    def _():
        wait_scatter(1 - slot, nv_ref[jnp.maximum(t - 1, 0)])
        wait_scatter(slot, nv)


def _moe(x_ext, row_token, tile_ea, tile_eb, tile_nv, g2, wg, wu, wd):
    n = x_ext.shape[0]
    n_tiles = row_token.shape[0]
    wspec_a = lambda shape: pl.BlockSpec(shape, lambda t, ea, eb, nv: (ea[t], 0, 0))
    wspec_b = lambda shape: pl.BlockSpec(shape, lambda t, ea, eb, nv: (eb[t], 0, 0))
    grid_spec = pltpu.PrefetchScalarGridSpec(
        num_scalar_prefetch=3,
        grid=(n_tiles,),
        in_specs=[
            pl.BlockSpec((1, 1, TM_MOE), lambda t, ea, eb, nv: (t, 0, 0), memory_space=pltpu.SMEM),
            pl.BlockSpec(memory_space=pl.ANY),
            pl.BlockSpec((1, D_MODEL), lambda t, ea, eb, nv: (0, 0)),
            wspec_a((1, D_MODEL, EXPERT_FF)), wspec_a((1, D_MODEL, EXPERT_FF)), wspec_a((1, EXPERT_FF, D_MODEL)),
            wspec_b((1, D_MODEL, EXPERT_FF)), wspec_b((1, D_MODEL, EXPERT_FF)), wspec_b((1, EXPERT_FF, D_MODEL)),
        ],
        out_specs=pl.BlockSpec(memory_space=pl.ANY),
        scratch_shapes=[
            pltpu.VMEM((TM_MOE, ROW_W), F32),
            pltpu.VMEM((2, TM_MOE, D_MODEL), F32),
            pltpu.SemaphoreType.DMA(()),
            pltpu.SemaphoreType.DMA((2,)),
        ],
    )
    return pl.pallas_call(
        _moe_kernel,
        grid_spec=grid_spec,
        out_shape=jax.ShapeDtypeStruct((n, D_MODEL), F32),
        compiler_params=_cparams(("arbitrary",)),
        name="moe",
    )(tile_ea, tile_eb, tile_nv, row_token, x_ext, g2, wg, wu, wd, wg, wu, wd)


def _moe_schedule(bucket, n):
    n_tiles = n // TM_MOE + N_BUCKETS
    tok = jnp.arange(n, dtype=jnp.int32)
    order = jnp.sort(bucket * n + tok) % n
    counts = jnp.sum((bucket[:, None] == jnp.arange(N_BUCKETS, dtype=jnp.int32)[None, :]).astype(jnp.int32), axis=0)
    tiles_b = (counts + TM_MOE - 1) // TM_MOE
    tile_end = jnp.cumsum(tiles_b)
    tile_start = tile_end - tiles_b
    cstart = jnp.cumsum(counts) - counts
    n_used = tile_end[-1]
    t = jnp.arange(n_tiles, dtype=jnp.int32)
    tb = jnp.minimum(jnp.searchsorted(tile_end, t, side="right").astype(jnp.int32), N_BUCKETS - 1)
    tb = jnp.where(t < n_used, tb, tb[jnp.maximum(n_used - 1, 0)])
    off = (t - tile_start[tb]) * TM_MOE
    nvalid = jnp.where(t < n_used, jnp.clip(counts[tb] - off, 0, TM_MOE), 0).astype(jnp.int32)
    r = jnp.arange(TM_MOE, dtype=jnp.int32)
    base = cstart[tb] + off
    cidx = base[:, None] + jnp.where(r[None, :] < nvalid[:, None], r[None, :], 0)
    row_token = order[jnp.clip(cidx, 0, n - 1)].astype(jnp.int32)
    pa = jnp.array([p[0] for p in PAIRS], jnp.int32)
    pb = jnp.array([p[1] for p in PAIRS], jnp.int32)
    grp = tb // len(PAIRS)
    tile_ea = (grp * EXPERTS_PER_GROUP + pa[tb % len(PAIRS)]).astype(jnp.int32)
    tile_eb = (grp * EXPERTS_PER_GROUP + pb[tb % len(PAIRS)]).astype(jnp.int32)
    return row_token.reshape(n_tiles, 1, TM_MOE), tile_ea, tile_eb, nvalid


def _retention_constants(s):
    half = RET_DK // 2
    inv = 1.0 / (10000.0 ** jnp.linspace(0.0, 1.0, half, dtype=F32))
    ang = jnp.arange(s).astype(F32)[:, None] * inv[None, :]
    cos_t = jnp.tile(jnp.cos(ang), (1, HEADS))
    sin_t = jnp.tile(jnp.sin(ang), (1, HEADS))
    log_g = jnp.log1p(-(2.0 ** (-5.0 - jnp.arange(HEADS, dtype=F32))))
    nn = jnp.arange(CHUNK, dtype=F32)
    diff = nn[:, None] - nn[None, :]
    dmask = jnp.where(diff >= 0, jnp.exp(log_g[:, None, None] * jnp.maximum(diff, 0.0)), 0.0)
    q_decay = jnp.exp(log_g[:, None] * (nn + 1.0))
    k_decay = jnp.exp(log_g[:, None] * (CHUNK - 1.0 - nn))
    lane_head = (jnp.arange(QW) % 128) // 32
    qdec = q_decay.T[:, lane_head]
    kdec = k_decay.T[:, lane_head]
    chunk_decay = tuple(float(np.exp(np.log1p(-(2.0 ** (-5.0 - h))) * CHUNK)) for h in range(HEADS))
    return cos_t, sin_t, dmask, qdec, kdec, chunk_decay


def _proj_column_order():
    perm = np.arange(D_IN)
    c = np.arange(QW)
    src = ((c % 128) // 32) * RET_DK + (c // 128) * (RET_DK // 2) + (c % 32)
    perm[0:QW] = src
    perm[QW:2 * QW] = QW + src
    return perm


def kernel(x, mem, norm1_g, w_in, ret_norm_g, moba_qn_g, moba_kn_g, mem_norm_g, w_mem_kv, mem_qn_g, mem_kn_g,
           w_out, norm2_g, w_router_grp, b_router_grp, w_router_exp, b_router_exp, w_exp_gate, w_exp_up,
           w_exp_down):
    b, s, d = x.shape
    assert d == D_MODEL and s % CHUNK == 0 and (b * s) % TM_PROJ == 0 and s % TQ_MEM == 0
    n = b * s
    depth = w_in.shape[0]
    cos_t, sin_t, dmask, qdec, kdec, chunk_decay = _retention_constants(s)
    perm = _proj_column_order()
    head_of = np.arange(QW) // DH
    bmat = jnp.asarray((head_of[:, None] == head_of[None, :]).astype(np.float32) / DH, BF16)

    x2d = x.reshape(n, d)
    for l in range(depth):
        proj = _norm_proj(x2d, norm1_g[l][None, :], w_in[l][:, perm].astype(BF16))
        proj3 = proj.reshape(b, s, D_IN)
        y_ret = _retention(proj3, cos_t, sin_t, dmask, qdec, kdec, ret_norm_g[l], chunk_decay)
        y_moba = _moba(proj3, jnp.tile(moba_qn_g[l], HEADS)[None, :], jnp.tile(moba_kn_g[l], HEADS)[None, :], bmat)
        y_mem = _mem_attn(proj3, mem, mem_norm_g[l][None, :], w_mem_kv[l].astype(BF16),
                          jnp.tile(mem_qn_g[l], HEADS)[None, :], jnp.tile(mem_kn_g[l], HEADS)[None, :], bmat)

        wr_t = jnp.zeros((ROUTER_ROWS, d), F32)
        wr_t = wr_t.at[0:N_GROUPS].set(w_router_grp[l].T).at[EXP_ROW0:EXP_ROW0 + N_EXPERTS].set(w_router_exp[l].T)
        br = jnp.zeros((ROUTER_ROWS, 1), F32)
        br = br.at[0:N_GROUPS, 0].set(b_router_grp[l]).at[EXP_ROW0:EXP_ROW0 + N_EXPERTS, 0].set(b_router_exp[l])
        x_ext, route = _out_route(x2d, y_ret.reshape(n, -1), y_moba.reshape(n, -1), y_mem.reshape(n, -1),
                                  w_out[l].astype(BF16), norm2_g[l][None, :], wr_t, br)

        row_token, tile_ea, tile_eb, tile_nv = _moe_schedule(route[0].astype(jnp.int32), n)
        x2d = _moe(x_ext, row_token, tile_ea, tile_eb, tile_nv, norm2_g[l][None, :],
                   w_exp_gate[l].astype(BF16), w_exp_up[l].astype(BF16), w_exp_down[l].astype(BF16))
    return x2d.reshape(b, s, d)
```

```python
import functools

import numpy as np
import jax
import jax.numpy as jnp
from jax import lax
from jax.experimental import pallas as pl
from jax.experimental.pallas import tpu as pltpu

F32 = jnp.float32
BF16 = jnp.bfloat16

D_MODEL = 1024
MEM_LEN = 256
HEADS = 4
RET_DK = 64
RET_DV = 128
CHUNK = 256
MOBA_TOPK = 3
DH = 64
N_GROUPS = 4
EXPERTS_PER_GROUP = 4
N_EXPERTS = N_GROUPS * EXPERTS_PER_GROUP
EXPERT_FF = 512
EPS = 1e-6
D_IN = 2560
QW = HEADS * DH

COL_RQ, COL_RK, COL_MQ, COL_MK, COL_MV, COL_CQ = 0, 1, 6, 7, 8, 9
COL_RV512, COL_RG512 = 1, 2

PAIRS = ((0, 1), (0, 2), (0, 3), (1, 2), (1, 3), (2, 3))
N_BUCKETS = N_GROUPS * len(PAIRS)
EXT = 128
ROW_W = D_MODEL + EXT

TM_PROJ = 512
TM_OUT = 512
TQ_MEM = 512
TM_MOE = 256
VMEM_LIMIT = 48 * 1024 * 1024
NEG_INF = float("-inf")


def _cparams(sem):
    return pltpu.CompilerParams(dimension_semantics=sem, vmem_limit_bytes=VMEM_LIMIT)


def _dot(a, b):
    return jnp.dot(a, b, preferred_element_type=F32)


def _dot_nt(a, b, precision=None):
    return lax.dot_general(a, b, (((1,), (1,)), ((), ())), precision=precision,
                           preferred_element_type=F32)


def _dot_tn(a, b):
    return lax.dot_general(a, b, (((0,), (0,)), ((), ())), preferred_element_type=F32)


def _head_mean_sq(x, bmat):
    x2 = x * x
    hi = x2.astype(BF16)
    lo = (x2 - hi.astype(F32)).astype(BF16)
    return _dot(hi, bmat) + _dot(lo, bmat)


def _lane_head_mask(width, head, group):
    lane = lax.broadcasted_iota(jnp.int32, (1, width), 1)
    return jnp.where((lane // group) == head, 1.0, 0.0).astype(F32)


def _norm_proj_kernel(x_ref, g_ref, w_ref, o_ref):
    x = x_ref[...]
    ms = jnp.mean(x * x, axis=-1, keepdims=True)
    h = (x * lax.rsqrt(ms + EPS) * g_ref[...]).astype(BF16)
    for c in range(D_IN // 512):
        o_ref[:, c * 512:(c + 1) * 512] = _dot(h, w_ref[:, c * 512:(c + 1) * 512]).astype(BF16)


def _norm_proj(x2d, g, w_bf16):
    n = x2d.shape[0]
    return pl.pallas_call(
        _norm_proj_kernel,
        grid=(n // TM_PROJ,),
        in_specs=[
            pl.BlockSpec((TM_PROJ, D_MODEL), lambda i: (i, 0)),
            pl.BlockSpec((1, D_MODEL), lambda i: (0, 0)),
            pl.BlockSpec((D_MODEL, D_IN), lambda i: (0, 0)),
        ],
        out_specs=pl.BlockSpec((TM_PROJ, D_IN), lambda i: (i, 0)),
        out_shape=jax.ShapeDtypeStruct((n, D_IN), BF16),
        compiler_params=_cparams(("parallel",)),
        name="norm_proj",
    )(x2d, g, w_bf16)


def _retention_kernel(chunk_decay, q_ref, k_ref, v_ref, g_ref, cos_ref, sin_ref, dmask_ref,
                      qdec_ref, kdec_ref, gn_ref, o_ref, state_ref):
    n = pl.program_id(1)

    @pl.when(n == 0)
    def _():
        state_ref[...] = jnp.zeros_like(state_ref)

    cos = cos_ref[...]
    sin = sin_ref[...]

    def rot(t):
        t1, t2 = t[:, :128], t[:, 128:]
        return jnp.concatenate([t1 * cos - t2 * sin, t2 * cos + t1 * sin], axis=1)

    qr = rot(q_ref[0].astype(F32))
    kr = rot(k_ref[0].astype(F32)) * (RET_DK ** -0.5)
    qb = qr.astype(BF16)
    qd = (qr * qdec_ref[...]).astype(BF16)
    kd = kr * kdec_ref[...]
    lane = lax.broadcasted_iota(jnp.int32, (1, QW), 1)
    for h in range(HEADS):
        hm = jnp.where(((lane % 128) // 32) == h, 1.0, 0.0).astype(F32)
        km = (kr * hm).astype(BF16)
        vh = v_ref[0, :, h * RET_DV:(h + 1) * RET_DV]
        s = _dot_nt(qb, km) * dmask_ref[h]
        o = _dot(s.astype(BF16), vh)
        o = o + _dot(qd, state_ref[h].astype(BF16))
        upd = _dot_tn((kd * hm).astype(BF16), vh)
        state_ref[h] = state_ref[h] * chunk_decay[h] + upd
        ms = jnp.mean(o * o, axis=-1, keepdims=True)
        ro = o * lax.rsqrt(ms + EPS) * gn_ref[h:h + 1, :]
        gate = g_ref[0, :, h * RET_DV:(h + 1) * RET_DV].astype(F32)
        o_ref[0, :, h * RET_DV:(h + 1) * RET_DV] = (gate * jax.nn.sigmoid(gate) * ro).astype(BF16)


def _retention(proj3, cos_t, sin_t, dmask, qdec, kdec, gn, chunk_decay):
    b, s, _ = proj3.shape
    nc = s // CHUNK
    return pl.pallas_call(
        functools.partial(_retention_kernel, chunk_decay),
        grid=(b, nc),
        in_specs=[
            pl.BlockSpec((1, CHUNK, QW), lambda bi, n: (bi, n, COL_RQ)),
            pl.BlockSpec((1, CHUNK, QW), lambda bi, n: (bi, n, COL_RK)),
            pl.BlockSpec((1, CHUNK, 512), lambda bi, n: (bi, n, COL_RV512)),
            pl.BlockSpec((1, CHUNK, 512), lambda bi, n: (bi, n, COL_RG512)),
            pl.BlockSpec((CHUNK, 128), lambda bi, n: (n, 0)),
            pl.BlockSpec((CHUNK, 128), lambda bi, n: (n, 0)),
            pl.BlockSpec((HEADS, CHUNK, CHUNK), lambda bi, n: (0, 0, 0)),
            pl.BlockSpec((CHUNK, QW), lambda bi, n: (0, 0)),
            pl.BlockSpec((CHUNK, QW), lambda bi, n: (0, 0)),
            pl.BlockSpec((HEADS, RET_DV), lambda bi, n: (0, 0)),
        ],
        out_specs=pl.BlockSpec((1, CHUNK, HEADS * RET_DV), lambda bi, n: (bi, n, 0)),
        out_shape=jax.ShapeDtypeStruct((b, s, HEADS * RET_DV), BF16),
        scratch_shapes=[pltpu.VMEM((HEADS, QW, RET_DV), F32)],
        compiler_params=_cparams(("parallel", "arbitrary")),
        name="retention",
    )(proj3, proj3, proj3, proj3, cos_t, sin_t, dmask, qdec, kdec, gn)


def _moba_kernel(q_ref, k_ref, v_ref, gq_ref, gk_ref, bmat_ref, o_ref,
                 khat_ref, kmean_ref, qm_ref, sel_ref):
    i = pl.program_id(1)
    nb = khat_ref.shape[0] // CHUNK
    bmat = bmat_ref[...]
    lane = lax.broadcasted_iota(jnp.int32, (1, QW), 1)
    hmasks = [jnp.where((lane // DH) == h, 1.0, 0.0).astype(F32) for h in range(HEADS)]

    @pl.when(i == 0)
    def _():
        for n in range(nb):
            kb = k_ref[0, n * CHUNK:(n + 1) * CHUNK, :].astype(F32)
            kh = kb * lax.rsqrt(_head_mean_sq(kb, bmat) + EPS) * gk_ref[...]
            khat_ref[n * CHUNK:(n + 1) * CHUNK, :] = kh.astype(BF16)
            km = jnp.mean(kh, axis=0, keepdims=True)
            for h in range(HEADS):
                kmean_ref[h * nb + n:h * nb + n + 1, :] = km * hmasks[h]

    q = q_ref[0].astype(F32)
    qn = q * lax.rsqrt(_head_mean_sq(q, bmat) + EPS) * gq_ref[...]
    for h in range(HEADS):
        qm_ref[h] = (qn * (hmasks[h] * (DH ** -0.5))).astype(BF16)

    gate = _dot_nt(kmean_ref[...], qn, precision=lax.Precision.HIGHEST)
    n_iota = lax.broadcasted_iota(jnp.int32, (nb, CHUNK), 0)
    valid = n_iota < i
    sel_rows = []
    for h in range(HEADS):
        gm = jnp.where(valid, gate[h * nb:(h + 1) * nb], NEG_INF)
        cnt = jnp.zeros((nb, CHUNK), F32)
        for n in range(nb):
            gn = gm[n:n + 1, :]
            beats = jnp.where(gn > gm, 1.0, jnp.where(gn == gm, jnp.where(n < n_iota, 1.0, 0.0), 0.0))
            cnt = cnt + beats
        sel_rows.append(jnp.where(valid, jnp.where(cnt < MOBA_TOPK, 1.0, 0.0), 0.0))
    sel_t = jnp.concatenate(sel_rows + [jnp.zeros((128 - HEADS * nb, CHUNK), F32)], axis=0)
    sel_ref[...] = sel_t.T

    row = lax.broadcasted_iota(jnp.int32, (CHUNK, CHUNK), 0)
    col = lax.broadcasted_iota(jnp.int32, (CHUNK, CHUNK), 1)
    causal = col <= row

    for c in range(nb):
        @pl.when(i == c)
        def _(c=c):
            nk = (c + 1) * CHUNK
            kall = khat_ref[0:nk, :]
            vall = v_ref[0, 0:nk, :]
            out = jnp.zeros((CHUNK, QW), F32)
            for h in range(HEADS):
                s = _dot_nt(qm_ref[h], kall)
                parts = []
                for j in range(c):
                    sc = h * nb + j
                    parts.append(jnp.where(sel_ref[:, sc:sc + 1] > 0.5, s[:, j * CHUNK:(j + 1) * CHUNK], NEG_INF))
                parts.append(jnp.where(causal, s[:, c * CHUNK:], NEG_INF))
                m = jnp.max(functools.reduce(jnp.maximum, parts), axis=-1, keepdims=True)
                ps = [jnp.exp(p - m) for p in parts]
                l = jnp.sum(functools.reduce(jnp.add, ps), axis=-1, keepdims=True)
                p = jnp.concatenate(ps, axis=1).astype(BF16) if c else ps[0].astype(BF16)
                out = out + hmasks[h] * (_dot(p, vall) / l)
            o_ref[0] = out.astype(BF16)


def _moba(proj3, gq, gk, bmat):
    b, s, _ = proj3.shape
    nb = s // CHUNK
    return pl.pallas_call(
        _moba_kernel,
        grid=(b, nb),
        in_specs=[
            pl.BlockSpec((1, CHUNK, QW), lambda bi, i: (bi, i, COL_MQ)),
            pl.BlockSpec((1, s, QW), lambda bi, i: (bi, 0, COL_MK)),
            pl.BlockSpec((1, s, QW), lambda bi, i: (bi, 0, COL_MV)),
            pl.BlockSpec((1, QW), lambda bi, i: (0, 0)),
            pl.BlockSpec((1, QW), lambda bi, i: (0, 0)),
            pl.BlockSpec((QW, QW), lambda bi, i: (0, 0)),
        ],
        out_specs=pl.BlockSpec((1, CHUNK, QW), lambda bi, i: (bi, i, 0)),
        out_shape=jax.ShapeDtypeStruct((b, s, QW), BF16),
        scratch_shapes=[
            pltpu.VMEM((s, QW), BF16),
            pltpu.VMEM((HEADS * nb, QW), F32),
            pltpu.VMEM((HEADS, CHUNK, QW), BF16),
            pltpu.VMEM((CHUNK, 128), F32),
        ],
        compiler_params=_cparams(("parallel", "arbitrary")),
        name="moba",
    )(proj3, proj3, proj3, gq, gk, bmat)


def _mem_attn_kernel(q_ref, mem_ref, gm_ref, w_ref, gq_ref, gk_ref, bmat_ref, o_ref, ck_ref, cv_ref):
    i = pl.program_id(1)
    bmat = bmat_ref[...]

    @pl.when(i == 0)
    def _():
        mem = mem_ref[0]
        ms = jnp.mean(mem * mem, axis=-1, keepdims=True)
        mn = (mem * lax.rsqrt(ms + EPS) * gm_ref[...]).astype(BF16)
        kv = _dot(mn, w_ref[...])
        ck = kv[:, :QW]
        ck_ref[...] = (ck * lax.rsqrt(_head_mean_sq(ck, bmat) + EPS) * gk_ref[...]).astype(BF16)
        cv_ref[...] = kv[:, QW:].astype(BF16)

    q = q_ref[0].astype(F32)
    qn = q * lax.rsqrt(_head_mean_sq(q, bmat) + EPS) * gq_ref[...]
    lane = lax.broadcasted_iota(jnp.int32, (1, QW), 1)
    out = jnp.zeros(qn.shape, F32)
    for h in range(HEADS):
        hm = jnp.where((lane // DH) == h, 1.0, 0.0).astype(F32)
        s = _dot_nt((qn * (hm * (DH ** -0.5))).astype(BF16), ck_ref[...])
        p = jnp.exp(s - jnp.max(s, axis=-1, keepdims=True))
        l = jnp.sum(p, axis=-1, keepdims=True)
        out = out + hm * (_dot(p.astype(BF16), cv_ref[...]) / l)
    o_ref[0] = out.astype(BF16)


def _mem_attn(proj3, mem, gm, w_bf16, gq, gk, bmat):
    b, s, _ = proj3.shape
    return pl.pallas_call(
        _mem_attn_kernel,
        grid=(b, s // TQ_MEM),
        in_specs=[
            pl.BlockSpec((1, TQ_MEM, QW), lambda bi, i: (bi, i, COL_CQ)),
            pl.BlockSpec((1, MEM_LEN, D_MODEL), lambda bi, i: (bi, 0, 0)),
            pl.BlockSpec((1, D_MODEL), lambda bi, i: (0, 0)),
            pl.BlockSpec((D_MODEL, 2 * QW), lambda bi, i: (0, 0)),
            pl.BlockSpec((1, QW), lambda bi, i: (0, 0)),
            pl.BlockSpec((1, QW), lambda bi, i: (0, 0)),
            pl.BlockSpec((QW, QW), lambda bi, i: (0, 0)),
        ],
        out_specs=pl.BlockSpec((1, TQ_MEM, QW), lambda bi, i: (bi, i, 0)),
        out_shape=jax.ShapeDtypeStruct((b, s, QW), BF16),
        scratch_shapes=[pltpu.VMEM((MEM_LEN, QW), BF16), pltpu.VMEM((MEM_LEN, QW), BF16)],
        compiler_params=_cparams(("parallel", "arbitrary")),
        name="mem_attn",
    )(proj3, mem, gm, w_bf16, gq, gk, bmat)


ROUTER_ROWS = 32
EXP_ROW0 = 8


def _out_route_kernel(x_ref, yr_ref, ym_ref, yc_ref, wo_ref, g2_ref, wr_ref, br_ref, xe_ref, rt_ref):
    x1 = x_ref[...]
    x1 = x1 + _dot(yr_ref[...], wo_ref[0:512, :])
    x1 = x1 + _dot(ym_ref[...], wo_ref[512:768, :])
    x1 = x1 + _dot(yc_ref[...], wo_ref[768:1024, :])
    ms = jnp.mean(x1 * x1, axis=-1, keepdims=True)
    xn = x1 * lax.rsqrt(ms + EPS) * g2_ref[...]
    lt = _dot_nt(wr_ref[...], xn, precision=lax.Precision.HIGHEST) + br_ref[...]
    tm = lt.shape[1]
    gl = lt[0:N_GROUPS]
    gmax = jnp.max(gl, axis=0, keepdims=True)
    g_iota = lax.broadcasted_iota(jnp.int32, (N_GROUPS, tm), 0)
    gidx = jnp.min(jnp.where(gl == gmax, g_iota, N_GROUPS), axis=0, keepdims=True)
    g_w = 1.0 / jnp.sum(jnp.exp(gl - gmax), axis=0, keepdims=True)
    e_in = jnp.zeros((EXPERTS_PER_GROUP, tm), F32)
    for g in range(N_GROUPS):
        r0 = EXP_ROW0 + g * EXPERTS_PER_GROUP
        e_in = jnp.where(gidx == g, lt[r0:r0 + EXPERTS_PER_GROUP], e_in)
    e_iota = lax.broadcasted_iota(jnp.int32, (EXPERTS_PER_GROUP, tm), 0)
    cnt = jnp.zeros((EXPERTS_PER_GROUP, tm), F32)
    for n in range(EXPERTS_PER_GROUP):
        en = e_in[n:n + 1, :]
        cnt = cnt + jnp.where(en > e_in, 1.0, jnp.where(en == e_in, jnp.where(n < e_iota, 1.0, 0.0), 0.0))
    sel = cnt < 2.0
    a_idx = jnp.min(jnp.where(sel, e_iota, EXPERTS_PER_GROUP), axis=0, keepdims=True)
    b_idx = jnp.max(jnp.where(sel, e_iota, -1), axis=0, keepdims=True)
    la = jnp.sum(jnp.where(e_iota == a_idx, e_in, 0.0), axis=0, keepdims=True)
    lb = jnp.sum(jnp.where(e_iota == b_idx, e_in, 0.0), axis=0, keepdims=True)
    mx = jnp.maximum(la, lb)
    ea = jnp.exp(la - mx)
    eb = jnp.exp(lb - mx)
    wa = g_w * (ea / (ea + eb))
    wb = g_w * (eb / (ea + eb))
    pair_off = jnp.where(a_idx == 0, 0, jnp.where(a_idx == 1, 3, 5))
    bucket = (gidx * len(PAIRS) + pair_off + (b_idx - a_idx - 1)).astype(F32)
    rows = jnp.concatenate([bucket, wa, wb, jnp.zeros((5, tm), F32)], axis=0)
    rt_ref[...] = rows
    ext = jnp.concatenate([rows, jnp.zeros((EXT - 8, tm), F32)], axis=0).T
    xe_ref[:, :D_MODEL] = x1
    xe_ref[:, D_MODEL:] = ext


def _out_route(x2d, y_ret, y_moba, y_mem, wo_bf16, g2, wr_t, br):
    n = x2d.shape[0]
    return pl.pallas_call(
        _out_route_kernel,
        grid=(n // TM_OUT,),
        in_specs=[
            pl.BlockSpec((TM_OUT, D_MODEL), lambda i: (i, 0)),
            pl.BlockSpec((TM_OUT, 512), lambda i: (i, 0)),
            pl.BlockSpec((TM_OUT, QW), lambda i: (i, 0)),
            pl.BlockSpec((TM_OUT, QW), lambda i: (i, 0)),
            pl.BlockSpec((D_MODEL, D_MODEL), lambda i: (0, 0)),
            pl.BlockSpec((1, D_MODEL), lambda i: (0, 0)),
            pl.BlockSpec((ROUTER_ROWS, D_MODEL), lambda i: (0, 0)),
            pl.BlockSpec((ROUTER_ROWS, 1), lambda i: (0, 0)),
        ],
        out_specs=[
            pl.BlockSpec((TM_OUT, ROW_W), lambda i: (i, 0)),
            pl.BlockSpec((8, TM_OUT), lambda i: (0, i)),
        ],
        out_shape=[jax.ShapeDtypeStruct((n, ROW_W), F32), jax.ShapeDtypeStruct((8, n), F32)],
        compiler_params=_cparams(("parallel",)),
        name="out_route",
    )(x2d, y_ret, y_moba, y_mem, wo_bf16, g2, wr_t, br)


def _moe_kernel(ea_ref, eb_ref, nv_ref, tok_ref, x_hbm, g2_ref, wga_ref, wua_ref, wda_ref,
                wgb_ref, wub_ref, wdb_ref, o_hbm, xbuf, obuf, gsem, ssem):
    t = pl.program_id(0)
    n_tiles = pl.num_programs(0)
    slot = t % 2

    def scatter_copy(s, r, tok):
        return pltpu.make_async_copy(obuf.at[s, pl.ds(r, 1)], o_hbm.at[pl.ds(tok, 1)], ssem.at[s])

    def wait_scatter(s, count):
        full = pl.multiple_of((count >> 3) << 3, 8)

        @pl.when(full > 0)
        def _():
            pltpu.make_async_copy(obuf.at[s, pl.ds(0, full)], o_hbm.at[pl.ds(0, full)], ssem.at[s]).wait()

        for p in (4, 2, 1):
            @pl.when((count & p) != 0)
            def _():
                for _i in range(p):
                    scatter_copy(s, 0, 0).wait()

    @pl.when(t >= 2)
    def _():
        wait_scatter(slot, nv_ref[jnp.maximum(t - 2, 0)])

    nv = nv_ref[t]

    @pl.when(nv > 0)
    def _():
        def issue(r, carry):
            tok = tok_ref[0, 0, r]
            pltpu.make_async_copy(x_hbm.at[pl.ds(tok, 1)], xbuf.at[pl.ds(r, 1)], gsem).start()
            return carry

        lax.fori_loop(0, TM_MOE, issue, 0)
        pltpu.make_async_copy(x_hbm.at[pl.ds(0, TM_MOE)], xbuf, gsem).wait()

        x1 = xbuf[:, :D_MODEL]
        wa = xbuf[:, D_MODEL + 1:D_MODEL + 2]
        wb = xbuf[:, D_MODEL + 2:D_MODEL + 3]
        ms = jnp.mean(x1 * x1, axis=-1, keepdims=True)
        xn = (x1 * lax.rsqrt(ms + EPS) * g2_ref[...]).astype(BF16)

        def expert(wg_ref, wu_ref, wd_ref):
            g = _dot(xn, wg_ref[0])
            u = _dot(xn, wu_ref[0])
            hid = (g * jax.nn.sigmoid(g) * u).astype(BF16)
            return _dot(hid, wd_ref[0])

        y = x1 + wa * expert(wga_ref, wua_ref, wda_ref)
        y = y + wb * expert(wgb_ref, wub_ref, wdb_ref)
        obuf[slot] = y

        def emit(r, carry):
            scatter_copy(slot, r, tok_ref[0, 0, r]).start()
            return carry

        lax.fori_loop(0, nv, emit, 0)

    @pl.when(t == n_tiles - 1)
    def _():
        wait_scatter(1 - slot, nv_ref[jnp.maximum(t - 1, 0)])
        wait_scatter(slot, nv)


def _moe(x_ext, row_token, tile_ea, tile_eb, tile_nv, g2, wg, wu, wd):
    n = x_ext.shape[0]
    n_tiles = row_token.shape[0]
    wspec_a = lambda shape: pl.BlockSpec(shape, lambda t, ea, eb, nv: (ea[t], 0, 0))
    wspec_b = lambda shape: pl.BlockSpec(shape, lambda t, ea, eb, nv: (eb[t], 0, 0))
    grid_spec = pltpu.PrefetchScalarGridSpec(
        num_scalar_prefetch=3,
        grid=(n_tiles,),
        in_specs=[
            pl.BlockSpec((1, 1, TM_MOE), lambda t, ea, eb, nv: (t, 0, 0), memory_space=pltpu.SMEM),
            pl.BlockSpec(memory_space=pl.ANY),
            pl.BlockSpec((1, D_MODEL), lambda t, ea, eb, nv: (0, 0)),
            wspec_a((1, D_MODEL, EXPERT_FF)), wspec_a((1, D_MODEL, EXPERT_FF)), wspec_a((1, EXPERT_FF, D_MODEL)),
            wspec_b((1, D_MODEL, EXPERT_FF)), wspec_b((1, D_MODEL, EXPERT_FF)), wspec_b((1, EXPERT_FF, D_MODEL)),
        ],
        out_specs=pl.BlockSpec(memory_space=pl.ANY),
        scratch_shapes=[
            pltpu.VMEM((TM_MOE, ROW_W), F32),
            pltpu.VMEM((2, TM_MOE, D_MODEL), F32),
            pltpu.SemaphoreType.DMA(()),
            pltpu.SemaphoreType.DMA((2,)),
        ],
    )
    return pl.pallas_call(
        _moe_kernel,
        grid_spec=grid_spec,
        out_shape=jax.ShapeDtypeStruct((n, D_MODEL), F32),
        compiler_params=_cparams(("arbitrary",)),
        name="moe",
    )(tile_ea, tile_eb, tile_nv, row_token, x_ext, g2, wg, wu, wd, wg, wu, wd)


def _moe_schedule(bucket, n):
    n_tiles = n // TM_MOE + N_BUCKETS
    tok = jnp.arange(n, dtype=jnp.int32)
    order = jnp.sort(bucket * n + tok) % n
    counts = jnp.sum((bucket[:, None] == jnp.arange(N_BUCKETS, dtype=jnp.int32)[None, :]).astype(jnp.int32), axis=0)
    tiles_b = (counts + TM_MOE - 1) // TM_MOE
    tile_end = jnp.cumsum(tiles_b)
    tile_start = tile_end - tiles_b
    cstart = jnp.cumsum(counts) - counts
    n_used = tile_end[-1]
    t = jnp.arange(n_tiles, dtype=jnp.int32)
    tb = jnp.minimum(jnp.searchsorted(tile_end, t, side="right").astype(jnp.int32), N_BUCKETS - 1)
    tb = jnp.where(t < n_used, tb, tb[jnp.maximum(n_used - 1, 0)])
    off = (t - tile_start[tb]) * TM_MOE
    nvalid = jnp.where(t < n_used, jnp.clip(counts[tb] - off, 0, TM_MOE), 0).astype(jnp.int32)
    r = jnp.arange(TM_MOE, dtype=jnp.int32)
    base = cstart[tb] + off
    cidx = base[:, None] + jnp.where(r[None, :] < nvalid[:, None], r[None, :], 0)
    row_token = order[jnp.clip(cidx, 0, n - 1)].astype(jnp.int32)
    pa = jnp.array([p[0] for p in PAIRS], jnp.int32)
    pb = jnp.array([p[1] for p in PAIRS], jnp.int32)
    grp = tb // len(PAIRS)
    tile_ea = (grp * EXPERTS_PER_GROUP + pa[tb % len(PAIRS)]).astype(jnp.int32)
    tile_eb = (grp * EXPERTS_PER_GROUP + pb[tb % len(PAIRS)]).astype(jnp.int32)
    return row_token.reshape(n_tiles, 1, TM_MOE), tile_ea, tile_eb, nvalid


def _retention_constants(s):
    half = RET_DK // 2
    inv = 1.0 / (10000.0 ** jnp.linspace(0.0, 1.0, half, dtype=F32))
    ang = jnp.arange(s).astype(F32)[:, None] * inv[None, :]
    cos_t = jnp.tile(jnp.cos(ang), (1, HEADS))
    sin_t = jnp.tile(jnp.sin(ang), (1, HEADS))
    log_g = jnp.log1p(-(2.0 ** (-5.0 - jnp.arange(HEADS, dtype=F32))))
    nn = jnp.arange(CHUNK, dtype=F32)
    diff = nn[:, None] - nn[None, :]
    dmask = jnp.where(diff >= 0, jnp.exp(log_g[:, None, None] * jnp.maximum(diff, 0.0)), 0.0)
    q_decay = jnp.exp(log_g[:, None] * (nn + 1.0))
    k_decay = jnp.exp(log_g[:, None] * (CHUNK - 1.0 - nn))
    lane_head = (jnp.arange(QW) % 128) // 32
    qdec = q_decay.T[:, lane_head]
    kdec = k_decay.T[:, lane_head]
    chunk_decay = tuple(float(np.exp(np.log1p(-(2.0 ** (-5.0 - h))) * CHUNK)) for h in range(HEADS))
    return cos_t, sin_t, dmask, qdec, kdec, chunk_decay


def _proj_column_order():
    perm = np.arange(D_IN)
    c = np.arange(QW)
    src = ((c % 128) // 32) * RET_DK + (c // 128) * (RET_DK // 2) + (c % 32)
    perm[0:QW] = src
    perm[QW:2 * QW] = QW + src
    return perm


def kernel(x, mem, norm1_g, w_in, ret_norm_g, moba_qn_g, moba_kn_g, mem_norm_g, w_mem_kv, mem_qn_g, mem_kn_g,
           w_out, norm2_g, w_router_grp, b_router_grp, w_router_exp, b_router_exp, w_exp_gate, w_exp_up,
           w_exp_down):
    b, s, d = x.shape
    assert d == D_MODEL and s % CHUNK == 0 and (b * s) % TM_PROJ == 0 and s % TQ_MEM == 0
    n = b * s
    depth = w_in.shape[0]
    cos_t, sin_t, dmask, qdec, kdec, chunk_decay = _retention_constants(s)
    perm = _proj_column_order()
    head_of = np.arange(QW) // DH
    bmat = jnp.asarray((head_of[:, None] == head_of[None, :]).astype(np.float32) / DH, BF16)

    x2d = x.reshape(n, d)
    for l in range(depth):
        proj = _norm_proj(x2d, norm1_g[l][None, :], w_in[l][:, perm].astype(BF16))
        proj3 = proj.reshape(b, s, D_IN)
        y_ret = _retention(proj3, cos_t, sin_t, dmask, qdec, kdec, ret_norm_g[l], chunk_decay)
        y_moba = _moba(proj3, jnp.tile(moba_qn_g[l], HEADS)[None, :], jnp.tile(moba_kn_g[l], HEADS)[None, :], bmat)
        y_mem = _mem_attn(proj3, mem, mem_norm_g[l][None, :], w_mem_kv[l].astype(BF16),
                          jnp.tile(mem_qn_g[l], HEADS)[None, :], jnp.tile(mem_kn_g[l], HEADS)[None, :], bmat)

        wr_t = jnp.zeros((ROUTER_ROWS, d), F32)
        wr_t = wr_t.at[0:N_GROUPS].set(w_router_grp[l].T).at[EXP_ROW0:EXP_ROW0 + N_EXPERTS].set(w_router_exp[l].T)
        br = jnp.zeros((ROUTER_ROWS, 1), F32)
        br = br.at[0:N_GROUPS, 0].set(b_router_grp[l]).at[EXP_ROW0:EXP_ROW0 + N_EXPERTS, 0].set(b_router_exp[l])
        x_ext, route = _out_route(x2d, y_ret.reshape(n, -1), y_moba.reshape(n, -1), y_mem.reshape(n, -1),
                                  w_out[l].astype(BF16), norm2_g[l][None, :], wr_t, br)

        row_token, tile_ea, tile_eb, tile_nv = _moe_schedule(route[0].astype(jnp.int32), n)
        x2d = _moe(x_ext, row_token, tile_ea, tile_eb, tile_nv, norm2_g[l][None, :],
                   w_exp_gate[l].astype(BF16), w_exp_up[l].astype(BF16), w_exp_down[l].astype(BF16))
    return x2d.reshape(b, s, d)
```

```python
import functools

import numpy as np
import jax
import jax.numpy as jnp
from jax import lax
from jax.experimental import pallas as pl
from jax.experimental.pallas import tpu as pltpu

F32 = jnp.float32
BF16 = jnp.bfloat16

D_MODEL = 1024
MEM_LEN = 256
HEADS = 4
RET_DK = 64
RET_DV = 128
CHUNK = 256
MOBA_TOPK = 3
DH = 64
N_GROUPS = 4
EXPERTS_PER_GROUP = 4
N_EXPERTS = N_GROUPS * EXPERTS_PER_GROUP
EXPERT_FF = 512
EPS = 1e-6
D_IN = 2560
QW = HEADS * DH

COL_RQ, COL_RK, COL_MQ, COL_MK, COL_MV, COL_CQ = 0, 1, 6, 7, 8, 9
COL_RV512, COL_RG512 = 1, 2

PAIRS = ((0, 1), (0, 2), (0, 3), (1, 2), (1, 3), (2, 3))
N_BUCKETS = N_GROUPS * len(PAIRS)
EXT = 128
ROW_W = D_MODEL + EXT

TM_PROJ = 512
TM_OUT = 512
TQ_MEM = 512
TM_MOE = 256
VMEM_LIMIT = 48 * 1024 * 1024
NEG_INF = float("-inf")


def _cparams(sem):
    return pltpu.CompilerParams(dimension_semantics=sem, vmem_limit_bytes=VMEM_LIMIT)


def _dot(a, b):
    return jnp.dot(a, b, preferred_element_type=F32)


def _dot_nt(a, b, precision=None):
    return lax.dot_general(a, b, (((1,), (1,)), ((), ())), precision=precision,
                           preferred_element_type=F32)


def _dot_tn(a, b):
    return lax.dot_general(a, b, (((0,), (0,)), ((), ())), preferred_element_type=F32)


def _head_mean_sq(x, bmat):
    x2 = x * x
    hi = x2.astype(BF16)
    lo = (x2 - hi.astype(F32)).astype(BF16)
    return _dot(hi, bmat) + _dot(lo, bmat)


def _norm_proj_kernel(x_ref, g_ref, w_ref, o_ref):
    x = x_ref[...]
    ms = jnp.mean(x * x, axis=-1, keepdims=True)
    h = (x * lax.rsqrt(ms + EPS) * g_ref[...]).astype(BF16)
    for c in range(D_IN // 512):
        o_ref[:, c * 512:(c + 1) * 512] = _dot(h, w_ref[:, c * 512:(c + 1) * 512]).astype(BF16)


def _norm_proj(x2d, g, w_bf16):
    n = x2d.shape[0]
    return pl.pallas_call(
        _norm_proj_kernel,
        grid=(n // TM_PROJ,),
        in_specs=[
            pl.BlockSpec((TM_PROJ, D_MODEL), lambda i: (i, 0)),
            pl.BlockSpec((1, D_MODEL), lambda i: (0, 0)),
            pl.BlockSpec((D_MODEL, D_IN), lambda i: (0, 0)),
        ],
        out_specs=pl.BlockSpec((TM_PROJ, D_IN), lambda i: (i, 0)),
        out_shape=jax.ShapeDtypeStruct((n, D_IN), BF16),
        compiler_params=_cparams(("parallel",)),
        name="norm_proj",
    )(x2d, g, w_bf16)


def _retention_kernel(chunk_decay, q_ref, k_ref, v_ref, g_ref, cos_ref, sin_ref, dmask_ref,
                      qdec_ref, kdec_ref, gn_ref, o_ref, state_ref):
    n = pl.program_id(1)

    @pl.when(n == 0)
    def _():
        state_ref[...] = jnp.zeros_like(state_ref)

    cos = cos_ref[...]
    sin = sin_ref[...]

    def rot(t):
        t1, t2 = t[:, :128], t[:, 128:]
        return jnp.concatenate([t1 * cos - t2 * sin, t2 * cos + t1 * sin], axis=1)

    qr = rot(q_ref[0].astype(F32))
    kr = rot(k_ref[0].astype(F32)) * (RET_DK ** -0.5)
    qb = qr.astype(BF16)
    qd = (qr * qdec_ref[...]).astype(BF16)
    kd = kr * kdec_ref[...]
    lane = lax.broadcasted_iota(jnp.int32, (1, QW), 1)
    for h in range(HEADS):
        hm = jnp.where(((lane % 128) // 32) == h, 1.0, 0.0).astype(F32)
        km = (kr * hm).astype(BF16)
        vh = v_ref[0, :, h * RET_DV:(h + 1) * RET_DV]
        s = _dot_nt(qb, km) * dmask_ref[h]
        o = _dot(s.astype(BF16), vh)
        o = o + _dot(qd, state_ref[h].astype(BF16))
        upd = _dot_tn((kd * hm).astype(BF16), vh)
        state_ref[h] = state_ref[h] * chunk_decay[h] + upd
        ms = jnp.mean(o * o, axis=-1, keepdims=True)
        ro = o * lax.rsqrt(ms + EPS) * gn_ref[h:h + 1, :]
        gate = g_ref[0, :, h * RET_DV:(h + 1) * RET_DV].astype(F32)
        o_ref[0, :, h * RET_DV:(h + 1) * RET_DV] = (gate * jax.nn.sigmoid(gate) * ro).astype(BF16)


def _retention(proj3, cos_t, sin_t, dmask, qdec, kdec, gn, chunk_decay):
    b, s, _ = proj3.shape
    nc = s // CHUNK
    return pl.pallas_call(
        functools.partial(_retention_kernel, chunk_decay),
        grid=(b, nc),
        in_specs=[
            pl.BlockSpec((1, CHUNK, QW), lambda bi, n: (bi, n, COL_RQ)),
            pl.BlockSpec((1, CHUNK, QW), lambda bi, n: (bi, n, COL_RK)),
            pl.BlockSpec((1, CHUNK, 512), lambda bi, n: (bi, n, COL_RV512)),
            pl.BlockSpec((1, CHUNK, 512), lambda bi, n: (bi, n, COL_RG512)),
            pl.BlockSpec((CHUNK, 128), lambda bi, n: (n, 0)),
            pl.BlockSpec((CHUNK, 128), lambda bi, n: (n, 0)),
            pl.BlockSpec((HEADS, CHUNK, CHUNK), lambda bi, n: (0, 0, 0)),
            pl.BlockSpec((CHUNK, QW), lambda bi, n: (0, 0)),
            pl.BlockSpec((CHUNK, QW), lambda bi, n: (0, 0)),
            pl.BlockSpec((HEADS, RET_DV), lambda bi, n: (0, 0)),
        ],
        out_specs=pl.BlockSpec((1, CHUNK, HEADS * RET_DV), lambda bi, n: (bi, n, 0)),
        out_shape=jax.ShapeDtypeStruct((b, s, HEADS * RET_DV), BF16),
        scratch_shapes=[pltpu.VMEM((HEADS, QW, RET_DV), F32)],
        compiler_params=_cparams(("parallel", "arbitrary")),
        name="retention",
    )(proj3, proj3, proj3, proj3, cos_t, sin_t, dmask, qdec, kdec, gn)


def _moba_kernel(q_ref, k_ref, v_ref, gq_ref, gk_ref, bmat_ref, o_ref,
                 khat_ref, kmean_ref, qm_ref, sel_ref):
    i = pl.program_id(1)
    nb = khat_ref.shape[0] // CHUNK
    bmat = bmat_ref[...]
    lane = lax.broadcasted_iota(jnp.int32, (1, QW), 1)
    hmasks = [jnp.where((lane // DH) == h, 1.0, 0.0).astype(F32) for h in range(HEADS)]

    @pl.when(i == 0)
    def _():
        for n in range(nb):
            kb = k_ref[0, n * CHUNK:(n + 1) * CHUNK, :].astype(F32)
            kh = kb * lax.rsqrt(_head_mean_sq(kb, bmat) + EPS) * gk_ref[...]
            khat_ref[n * CHUNK:(n + 1) * CHUNK, :] = kh.astype(BF16)
            km = jnp.mean(kh, axis=0, keepdims=True)
            for h in range(HEADS):
                kmean_ref[h * nb + n:h * nb + n + 1, :] = km * hmasks[h]

    q = q_ref[0].astype(F32)
    qn = q * lax.rsqrt(_head_mean_sq(q, bmat) + EPS) * gq_ref[...]
    for h in range(HEADS):
        qm_ref[h] = (qn * (hmasks[h] * (DH ** -0.5))).astype(BF16)

    gate = _dot_nt(kmean_ref[...], qn, precision=lax.Precision.HIGHEST)
    n_iota = lax.broadcasted_iota(jnp.int32, (nb, CHUNK), 0)
    valid = n_iota < i
    sel_rows = []
    for h in range(HEADS):
        gm = jnp.where(valid, gate[h * nb:(h + 1) * nb], NEG_INF)
        cnt = jnp.zeros((nb, CHUNK), F32)
        for n in range(nb):
            gn = gm[n:n + 1, :]
            beats = jnp.where(gn > gm, 1.0, jnp.where(gn == gm, jnp.where(n < n_iota, 1.0, 0.0), 0.0))
            cnt = cnt + beats
        sel_rows.append(jnp.where(valid, jnp.where(cnt < MOBA_TOPK, 1.0, 0.0), 0.0))
    sel_t = jnp.concatenate(sel_rows + [jnp.zeros((128 - HEADS * nb, CHUNK), F32)], axis=0)
    sel_ref[...] = sel_t.T

    row = lax.broadcasted_iota(jnp.int32, (CHUNK, CHUNK), 0)
    col = lax.broadcasted_iota(jnp.int32, (CHUNK, CHUNK), 1)
    causal = col <= row

    for c in range(nb):
        @pl.when(i == c)
        def _(c=c):
            nk = (c + 1) * CHUNK
            kall = khat_ref[0:nk, :]
            vall = v_ref[0, 0:nk, :]
            out = jnp.zeros((CHUNK, QW), F32)
            for h in range(HEADS):
                s = _dot_nt(qm_ref[h], kall)
                parts = []
                for j in range(c):
                    sc = h * nb + j
                    parts.append(jnp.where(sel_ref[:, sc:sc + 1] > 0.5, s[:, j * CHUNK:(j + 1) * CHUNK], NEG_INF))
                parts.append(jnp.where(causal, s[:, c * CHUNK:], NEG_INF))
                m = jnp.max(functools.reduce(jnp.maximum, parts), axis=-1, keepdims=True)
                ps = [jnp.exp(p - m) for p in parts]
                l = jnp.sum(functools.reduce(jnp.add, ps), axis=-1, keepdims=True)
                p = jnp.concatenate(ps, axis=1).astype(BF16) if c else ps[0].astype(BF16)
                out = out + hmasks[h] * (_dot(p, vall) / l)
            o_ref[0] = out.astype(BF16)


def _moba(proj3, gq, gk, bmat):
    b, s, _ = proj3.shape
    nb = s // CHUNK
    return pl.pallas_call(
        _moba_kernel,
        grid=(b, nb),
        in_specs=[
            pl.BlockSpec((1, CHUNK, QW), lambda bi, i: (bi, i, COL_MQ)),
            pl.BlockSpec((1, s, QW), lambda bi, i: (bi, 0, COL_MK)),
            pl.BlockSpec((1, s, QW), lambda bi, i: (bi, 0, COL_MV)),
            pl.BlockSpec((1, QW), lambda bi, i: (0, 0)),
            pl.BlockSpec((1, QW), lambda bi, i: (0, 0)),
            pl.BlockSpec((QW, QW), lambda bi, i: (0, 0)),
        ],
        out_specs=pl.BlockSpec((1, CHUNK, QW), lambda bi, i: (bi, i, 0)),
        out_shape=jax.ShapeDtypeStruct((b, s, QW), BF16),
        scratch_shapes=[
            pltpu.VMEM((s, QW), BF16),
            pltpu.VMEM((HEADS * nb, QW), F32),
            pltpu.VMEM((HEADS, CHUNK, QW), BF16),
            pltpu.VMEM((CHUNK, 128), F32),
        ],
        compiler_params=_cparams(("parallel", "arbitrary")),
        name="moba",
    )(proj3, proj3, proj3, gq, gk, bmat)


def _mem_attn_kernel(q_ref, mem_ref, gm_ref, w_ref, gq_ref, gk_ref, bmat_ref, o_ref, ck_ref, cv_ref):
    i = pl.program_id(1)
    bmat = bmat_ref[...]

    @pl.when(i == 0)
    def _():
        mem = mem_ref[0]
        ms = jnp.mean(mem * mem, axis=-1, keepdims=True)
        mn = (mem * lax.rsqrt(ms + EPS) * gm_ref[...]).astype(BF16)
        kv = _dot(mn, w_ref[...])
        ck = kv[:, :QW]
        ck_ref[...] = (ck * lax.rsqrt(_head_mean_sq(ck, bmat) + EPS) * gk_ref[...]).astype(BF16)
        cv_ref[...] = kv[:, QW:].astype(BF16)

    q = q_ref[0].astype(F32)
    qn = q * lax.rsqrt(_head_mean_sq(q, bmat) + EPS) * gq_ref[...]
    lane = lax.broadcasted_iota(jnp.int32, (1, QW), 1)
    out = jnp.zeros(qn.shape, F32)
    for h in range(HEADS):
        hm = jnp.where((lane // DH) == h, 1.0, 0.0).astype(F32)
        s = _dot_nt((qn * (hm * (DH ** -0.5))).astype(BF16), ck_ref[...])
        p = jnp.exp(s - jnp.max(s, axis=-1, keepdims=True))
        l = jnp.sum(p, axis=-1, keepdims=True)
        out = out + hm * (_dot(p.astype(BF16), cv_ref[...]) / l)
    o_ref[0] = out.astype(BF16)


def _mem_attn(proj3, mem, gm, w_bf16, gq, gk, bmat):
    b, s, _ = proj3.shape
    return pl.pallas_call(
        _mem_attn_kernel,
        grid=(b, s // TQ_MEM),
        in_specs=[
            pl.BlockSpec((1, TQ_MEM, QW), lambda bi, i: (bi, i, COL_CQ)),
            pl.BlockSpec((1, MEM_LEN, D_MODEL), lambda bi, i: (bi, 0, 0)),
            pl.BlockSpec((1, D_MODEL), lambda bi, i: (0, 0)),
            pl.BlockSpec((D_MODEL, 2 * QW), lambda bi, i: (0, 0)),
            pl.BlockSpec((1, QW), lambda bi, i: (0, 0)),
            pl.BlockSpec((1, QW), lambda bi, i: (0, 0)),
            pl.BlockSpec((QW, QW), lambda bi, i: (0, 0)),
        ],
        out_specs=pl.BlockSpec((1, TQ_MEM, QW), lambda bi, i: (bi, i, 0)),
        out_shape=jax.ShapeDtypeStruct((b, s, QW), BF16),
        scratch_shapes=[pltpu.VMEM((MEM_LEN, QW), BF16), pltpu.VMEM((MEM_LEN, QW), BF16)],
        compiler_params=_cparams(("parallel", "arbitrary")),
        name="mem_attn",
    )(proj3, mem, gm, w_bf16, gq, gk, bmat)


ROUTER_ROWS = 32
EXP_ROW0 = 8


def _out_route_kernel(x_ref, yr_ref, ym_ref, yc_ref, wo_ref, g2_ref, wr_ref, br_ref, xe_ref, rt_ref):
    x1 = x_ref[...]
    x1 = x1 + _dot(yr_ref[...], wo_ref[0:512, :])
    x1 = x1 + _dot(ym_ref[...], wo_ref[512:768, :])
    x1 = x1 + _dot(yc_ref[...], wo_ref[768:1024, :])
    ms = jnp.mean(x1 * x1, axis=-1, keepdims=True)
    xn = x1 * lax.rsqrt(ms + EPS) * g2_ref[...]
    lt = _dot_nt(wr_ref[...], xn, precision=lax.Precision.HIGHEST) + br_ref[...]
    tm = lt.shape[1]
    gl = lt[0:N_GROUPS]
    gmax = jnp.max(gl, axis=0, keepdims=True)
    g_iota = lax.broadcasted_iota(jnp.int32, (N_GROUPS, tm), 0)
    gidx = jnp.min(jnp.where(gl == gmax, g_iota, N_GROUPS), axis=0, keepdims=True)
    g_w = 1.0 / jnp.sum(jnp.exp(gl - gmax), axis=0, keepdims=True)
    e_in = jnp.zeros((EXPERTS_PER_GROUP, tm), F32)
    for g in range(N_GROUPS):
        r0 = EXP_ROW0 + g * EXPERTS_PER_GROUP
        e_in = jnp.where(gidx == g, lt[r0:r0 + EXPERTS_PER_GROUP], e_in)
    e_iota = lax.broadcasted_iota(jnp.int32, (EXPERTS_PER_GROUP, tm), 0)
    cnt = jnp.zeros((EXPERTS_PER_GROUP, tm), F32)
    for n in range(EXPERTS_PER_GROUP):
        en = e_in[n:n + 1, :]
        cnt = cnt + jnp.where(en > e_in, 1.0, jnp.where(en == e_in, jnp.where(n < e_iota, 1.0, 0.0), 0.0))
    sel = cnt < 2.0
    a_idx = jnp.min(jnp.where(sel, e_iota, EXPERTS_PER_GROUP), axis=0, keepdims=True)
    b_idx = jnp.max(jnp.where(sel, e_iota, -1), axis=0, keepdims=True)
    la = jnp.sum(jnp.where(e_iota == a_idx, e_in, 0.0), axis=0, keepdims=True)
    lb = jnp.sum(jnp.where(e_iota == b_idx, e_in, 0.0), axis=0, keepdims=True)
    mx = jnp.maximum(la, lb)
    ea = jnp.exp(la - mx)
    eb = jnp.exp(lb - mx)
    wa = g_w * (ea / (ea + eb))
    wb = g_w * (eb / (ea + eb))
    pair_off = jnp.where(a_idx == 0, 0, jnp.where(a_idx == 1, 3, 5))
    bucket = (gidx * len(PAIRS) + pair_off + (b_idx - a_idx - 1)).astype(F32)
    rows = jnp.concatenate([bucket, wa, wb, jnp.zeros((5, tm), F32)], axis=0)
    rt_ref[...] = rows
    ext = jnp.concatenate([rows, jnp.zeros((EXT - 8, tm), F32)], axis=0).T
    xe_ref[:, :D_MODEL] = x1
    xe_ref[:, D_MODEL:] = ext


def _out_route(x2d, y_ret, y_moba, y_mem, wo_bf16, g2, wr_t, br):
    n = x2d.shape[0]
    return pl.pallas_call(
        _out_route_kernel,
        grid=(n // TM_OUT,),
        in_specs=[
            pl.BlockSpec((TM_OUT, D_MODEL), lambda i: (i, 0)),
            pl.BlockSpec((TM_OUT, 512), lambda i: (i, 0)),
            pl.BlockSpec((TM_OUT, QW), lambda i: (i, 0)),
            pl.BlockSpec((TM_OUT, QW), lambda i: (i, 0)),
            pl.BlockSpec((D_MODEL, D_MODEL), lambda i: (0, 0)),
            pl.BlockSpec((1, D_MODEL), lambda i: (0, 0)),
            pl.BlockSpec((ROUTER_ROWS, D_MODEL), lambda i: (0, 0)),
            pl.BlockSpec((ROUTER_ROWS, 1), lambda i: (0, 0)),
        ],
        out_specs=[
            pl.BlockSpec((TM_OUT, ROW_W), lambda i: (i, 0)),
            pl.BlockSpec((8, TM_OUT), lambda i: (0, i)),
        ],
        out_shape=[jax.ShapeDtypeStruct((n, ROW_W), F32), jax.ShapeDtypeStruct((8, n), F32)],
        compiler_params=_cparams(("parallel",)),
        name="out_route",
    )(x2d, y_ret, y_moba, y_mem, wo_bf16, g2, wr_t, br)


def _moe_kernel(tile_ref, ea_ref, eb_ref, lo_ref, hi_ref, chg_ref,
                tokp_ref, tokc_ref, tokn_ref, x_hbm, g2_ref,
                wga_ref, wua_ref, wda_ref, wgb_ref, wub_ref, wdb_ref, o_hbm,
                xbuf, obuf, xn_ref, bga, bua, bda, bgb, bub, bdb, gsem, ssem):
    s = pl.program_id(0)
    n_steps = pl.num_programs(0)
    n_tiles = o_hbm.shape[0] // TM_MOE
    tile = tile_ref[s]
    lo = lo_ref[s]
    hi = hi_ref[s]
    slot = tile % 2
    valid = hi > lo
    first = jnp.logical_and(valid, lo == 0)

    def gather_start(tok_ref, sl):
        for r in range(TM_MOE):
            pltpu.make_async_copy(x_hbm.at[pl.ds(tok_ref[0, 0, r], 1)], xbuf.at[sl, pl.ds(r, 1)],
                                  gsem.at[sl]).start()

    def gather_wait(sl):
        pltpu.make_async_copy(x_hbm.at[pl.ds(0, TM_MOE)], xbuf.at[sl], gsem.at[sl]).wait()

    def scatter_start(tok_ref, sl):
        for r in range(TM_MOE):
            pltpu.make_async_copy(obuf.at[sl, pl.ds(r, 1)], o_hbm.at[pl.ds(tok_ref[0, 0, r], 1)],
                                  ssem.at[sl]).start()

    def scatter_wait(sl):
        pltpu.make_async_copy(obuf.at[sl], o_hbm.at[pl.ds(0, TM_MOE)], ssem.at[sl]).wait()

    @pl.when(s == 0)
    def _():
        gather_start(tokc_ref, 0)

    @pl.when(chg_ref[s] == 1)
    def _():
        bga[...] = wga_ref[0].astype(BF16)
        bua[...] = wua_ref[0].astype(BF16)
        bda[...] = wda_ref[0].astype(BF16)
        bgb[...] = wgb_ref[0].astype(BF16)
        bub[...] = wub_ref[0].astype(BF16)
        bdb[...] = wdb_ref[0].astype(BF16)

    def segment_out():
        xn = xn_ref[...]

        def expert(wg, wu, wd):
            g = _dot(xn, wg[...])
            u = _dot(xn, wu[...])
            return _dot((g * jax.nn.sigmoid(g) * u).astype(BF16), wd[...])

        wa = xbuf[slot, :, D_MODEL + 1:D_MODEL + 2]
        wb = xbuf[slot, :, D_MODEL + 2:D_MODEL + 3]
        y = wa * expert(bga, bua, bda) + wb * expert(bgb, bub, bdb)
        row = lax.broadcasted_iota(jnp.int32, (TM_MOE, 1), 0)
        return jnp.where(jnp.logical_and(row >= lo, row < hi), y, 0.0)

    def first_segment(next_gather, prev_scatter):
        if next_gather:
            gather_start(tokn_ref, 1 - slot)
        if prev_scatter:
            scatter_start(tokp_ref, 1 - slot)
        x1 = xbuf[slot, :, :D_MODEL]
        ms = jnp.mean(x1 * x1, axis=-1, keepdims=True)
        xn_ref[...] = (x1 * lax.rsqrt(ms + EPS) * g2_ref[...]).astype(BF16)
        obuf[slot] = x1 + segment_out()

    @pl.when(first)
    def _():
        @pl.when(tile >= 2)
        def _():
            scatter_wait(slot)

        gather_wait(slot)

    @pl.when(jnp.logical_and(first, tile == 0))
    def _():
        first_segment(True, False)

    @pl.when(jnp.logical_and(first, jnp.logical_and(tile > 0, tile < n_tiles - 1)))
    def _():
        first_segment(True, True)

    @pl.when(jnp.logical_and(first, tile == n_tiles - 1))
    def _():
        first_segment(False, True)

    @pl.when(jnp.logical_and(valid, lo > 0))
    def _():
        obuf[slot] = obuf[slot] + segment_out()

    @pl.when(s == n_steps - 1)
    def _():
        scatter_start(tokc_ref, (n_tiles - 1) % 2)
        scatter_wait(0)
        scatter_wait(1)


def _moe(x_ext, order, steps, g2, wg, wu, wd):
    n = x_ext.shape[0]
    n_tiles = n // TM_MOE
    st_tile, st_ea, st_eb, st_lo, st_hi, st_chg = steps
    tok = order.reshape(n_tiles, 1, TM_MOE)

    def tok_spec(shift):
        return pl.BlockSpec((1, 1, TM_MOE),
                            lambda s, tile, *_: (jnp.clip(tile[s] + shift, 0, n_tiles - 1), 0, 0),
                            memory_space=pltpu.SMEM)

    wspec_a = lambda shape: pl.BlockSpec(shape, lambda s, tile, ea, eb, *_: (ea[s], 0, 0))
    wspec_b = lambda shape: pl.BlockSpec(shape, lambda s, tile, ea, eb, *_: (eb[s], 0, 0))
    gu, dn = (D_MODEL, EXPERT_FF), (EXPERT_FF, D_MODEL)
    grid_spec = pltpu.PrefetchScalarGridSpec(
        num_scalar_prefetch=6,
        grid=(st_tile.shape[0],),
        in_specs=[
            tok_spec(-1), tok_spec(0), tok_spec(1),
            pl.BlockSpec(memory_space=pl.ANY),
            pl.BlockSpec((1, D_MODEL), lambda s, *_: (0, 0)),
            wspec_a((1,) + gu), wspec_a((1,) + gu), wspec_a((1,) + dn),
            wspec_b((1,) + gu), wspec_b((1,) + gu), wspec_b((1,) + dn),
        ],
        out_specs=pl.BlockSpec(memory_space=pl.ANY),
        scratch_shapes=[
            pltpu.VMEM((2, TM_MOE, ROW_W), F32),
            pltpu.VMEM((2, TM_MOE, D_MODEL), F32),
            pltpu.VMEM((TM_MOE, D_MODEL), BF16),
            pltpu.VMEM(gu, BF16), pltpu.VMEM(gu, BF16), pltpu.VMEM(dn, BF16),
            pltpu.VMEM(gu, BF16), pltpu.VMEM(gu, BF16), pltpu.VMEM(dn, BF16),
            pltpu.SemaphoreType.DMA((2,)),
            pltpu.SemaphoreType.DMA((2,)),
        ],
    )
    return pl.pallas_call(
        _moe_kernel,
        grid_spec=grid_spec,
        out_shape=jax.ShapeDtypeStruct((n, D_MODEL), F32),
        compiler_params=_cparams(("arbitrary",)),
        name="moe",
    )(st_tile, st_ea, st_eb, st_lo, st_hi, st_chg, tok, tok, tok, x_ext, g2, wg, wu, wd, wg, wu, wd)


def _moe_schedule(bucket, n):
    i32 = jnp.int32
    n_tiles = n // TM_MOE
    onehot = (bucket[:, None] == jnp.arange(N_BUCKETS, dtype=i32)[None, :]).astype(i32)
    csum = jnp.cumsum(onehot, axis=0)
    counts = csum[-1]
    cend = jnp.cumsum(counts)
    cstart = cend - counts
    pos = jnp.sum(onehot * (cstart[None, :] + csum - 1), axis=1)
    order = jnp.zeros((n,), i32).at[pos].set(jnp.arange(n, dtype=i32))

    v = jnp.concatenate([jnp.arange(n_tiles, dtype=i32) * TM_MOE, cstart])
    idx = jnp.arange(v.shape[0], dtype=i32)
    before = (v[None, :] < v[:, None]) | ((v[None, :] == v[:, None]) & (idx[None, :] < idx[:, None]))
    rank = jnp.sum(before.astype(i32), axis=1)
    starts = jnp.sum(jnp.where(rank[:, None] == idx[None, :], v[:, None], 0), axis=0)
    ends = jnp.concatenate([starts[1:], jnp.array([n], i32)])
    tile = jnp.minimum(starts // TM_MOE, n_tiles - 1)
    sb = jnp.minimum(jnp.sum((cend[None, :] <= starts[:, None]).astype(i32), axis=1), N_BUCKETS - 1)
    lo = starts - tile * TM_MOE
    hi = ends - tile * TM_MOE
    onehot_sb = (sb[:, None] == jnp.arange(N_BUCKETS, dtype=i32)[None, :]).astype(i32)
    chg = ((hi > lo) & (starts == jnp.sum(onehot_sb * cstart[None, :], axis=1))).astype(i32)
    pa = jnp.array([p[0] for p in PAIRS], i32)
    pb = jnp.array([p[1] for p in PAIRS], i32)
    grp = sb // len(PAIRS)
    ea = grp * EXPERTS_PER_GROUP + pa[sb % len(PAIRS)]
    eb = grp * EXPERTS_PER_GROUP + pb[sb % len(PAIRS)]
    return order, (tile.astype(i32), ea.astype(i32), eb.astype(i32), lo.astype(i32), hi.astype(i32), chg)


def _retention_constants(s):
    half = RET_DK // 2
    inv = 1.0 / (10000.0 ** jnp.linspace(0.0, 1.0, half, dtype=F32))
    ang = jnp.arange(s).astype(F32)[:, None] * inv[None, :]
    cos_t = jnp.tile(jnp.cos(ang), (1, HEADS))
    sin_t = jnp.tile(jnp.sin(ang), (1, HEADS))
    log_g = jnp.log1p(-(2.0 ** (-5.0 - jnp.arange(HEADS, dtype=F32))))
    nn = jnp.arange(CHUNK, dtype=F32)
    diff = nn[:, None] - nn[None, :]
    dmask = jnp.where(diff >= 0, jnp.exp(log_g[:, None, None] * jnp.maximum(diff, 0.0)), 0.0)
    q_decay = jnp.exp(log_g[:, None] * (nn + 1.0))
    k_decay = jnp.exp(log_g[:, None] * (CHUNK - 1.0 - nn))
    lane_head = (jnp.arange(QW) % 128) // 32
    qdec = q_decay.T[:, lane_head]
    kdec = k_decay.T[:, lane_head]
    chunk_decay = tuple(float(np.exp(np.log1p(-(2.0 ** (-5.0 - h))) * CHUNK)) for h in range(HEADS))
    return cos_t, sin_t, dmask, qdec, kdec, chunk_decay


def _proj_column_order():
    perm = np.arange(D_IN)
    c = np.arange(QW)
    src = ((c % 128) // 32) * RET_DK + (c // 128) * (RET_DK // 2) + (c % 32)
    perm[0:QW] = src
    perm[QW:2 * QW] = QW + src
    return perm


def kernel(x, mem, norm1_g, w_in, ret_norm_g, moba_qn_g, moba_kn_g, mem_norm_g, w_mem_kv, mem_qn_g, mem_kn_g,
           w_out, norm2_g, w_router_grp, b_router_grp, w_router_exp, b_router_exp, w_exp_gate, w_exp_up,
           w_exp_down):
    b, s, d = x.shape
    assert d == D_MODEL and s % CHUNK == 0 and (b * s) % TM_PROJ == 0 and s % TQ_MEM == 0
    n = b * s
    depth = w_in.shape[0]
    cos_t, sin_t, dmask, qdec, kdec, chunk_decay = _retention_constants(s)
    perm = _proj_column_order()
    head_of = np.arange(QW) // DH
    bmat = jnp.asarray((head_of[:, None] == head_of[None, :]).astype(np.float32) / DH, BF16)

    x2d = x.reshape(n, d)
    for l in range(depth):
        proj = _norm_proj(x2d, norm1_g[l][None, :], w_in[l][:, perm].astype(BF16))
        proj3 = proj.reshape(b, s, D_IN)
        y_ret = _retention(proj3, cos_t, sin_t, dmask, qdec, kdec, ret_norm_g[l], chunk_decay)
        y_moba = _moba(proj3, jnp.tile(moba_qn_g[l], HEADS)[None, :], jnp.tile(moba_kn_g[l], HEADS)[None, :], bmat)
        y_mem = _mem_attn(proj3, mem, mem_norm_g[l][None, :], w_mem_kv[l].astype(BF16),
                          jnp.tile(mem_qn_g[l], HEADS)[None, :], jnp.tile(mem_kn_g[l], HEADS)[None, :], bmat)

        wr_t = jnp.zeros((ROUTER_ROWS, d), F32)
        wr_t = wr_t.at[0:N_GROUPS].set(w_router_grp[l].T).at[EXP_ROW0:EXP_ROW0 + N_EXPERTS].set(w_router_exp[l].T)
        br = jnp.zeros((ROUTER_ROWS, 1), F32)
        br = br.at[0:N_GROUPS, 0].set(b_router_grp[l]).at[EXP_ROW0:EXP_ROW0 + N_EXPERTS, 0].set(b_router_exp[l])
        x_ext, route = _out_route(x2d, y_ret.reshape(n, -1), y_moba.reshape(n, -1), y_mem.reshape(n, -1),
                                  w_out[l].astype(BF16), norm2_g[l][None, :], wr_t, br)

        order, steps = _moe_schedule(route[0].astype(jnp.int32), n)
        x2d = _moe(x_ext, order, steps, norm2_g[l][None, :], w_exp_gate[l], w_exp_up[l], w_exp_down[l])
    return x2d.reshape(b, s, d)
```

```python
import functools

import numpy as np
import jax
import jax.numpy as jnp
from jax import lax
from jax.experimental import pallas as pl
from jax.experimental.pallas import tpu as pltpu

F32 = jnp.float32
BF16 = jnp.bfloat16

D_MODEL = 1024
MEM_LEN = 256
HEADS = 4
RET_DK = 64
RET_DV = 128
CHUNK = 256
MOBA_TOPK = 3
DH = 64
N_GROUPS = 4
EXPERTS_PER_GROUP = 4
N_EXPERTS = N_GROUPS * EXPERTS_PER_GROUP
EXPERT_FF = 512
EPS = 1e-6
D_IN = 2560
QW = HEADS * DH

COL_RQ, COL_RK, COL_MQ, COL_MK, COL_MV, COL_CQ = 0, 1, 6, 7, 8, 9
COL_RV512, COL_RG512 = 1, 2

PAIRS = ((0, 1), (0, 2), (0, 3), (1, 2), (1, 3), (2, 3))
N_BUCKETS = N_GROUPS * len(PAIRS)
EXT = 128
ROW_W = D_MODEL + EXT

TM_PROJ = 512
TM_OUT = 512
TQ_MEM = 512
TM_MOE = 256
VMEM_LIMIT = 48 * 1024 * 1024
NEG_INF = float("-inf")


def _cparams(sem):
    return pltpu.CompilerParams(dimension_semantics=sem, vmem_limit_bytes=VMEM_LIMIT)


def _dot(a, b):
    return jnp.dot(a, b, preferred_element_type=F32)


def _dot_nt(a, b, precision=None):
    return lax.dot_general(a, b, (((1,), (1,)), ((), ())), precision=precision,
                           preferred_element_type=F32)


def _dot_tn(a, b):
    return lax.dot_general(a, b, (((0,), (0,)), ((), ())), preferred_element_type=F32)


def _head_mean_sq(x, bmat):
    x2 = x * x
    hi = x2.astype(BF16)
    lo = (x2 - hi.astype(F32)).astype(BF16)
    return _dot(hi, bmat) + _dot(lo, bmat)


def _norm_proj_kernel(x_ref, g_ref, w_ref, o_ref):
    x = x_ref[...]
    ms = jnp.mean(x * x, axis=-1, keepdims=True)
    h = (x * lax.rsqrt(ms + EPS) * g_ref[...]).astype(BF16)
    for c in range(D_IN // 512):
        o_ref[:, c * 512:(c + 1) * 512] = _dot(h, w_ref[:, c * 512:(c + 1) * 512]).astype(BF16)


def _norm_proj(x2d, g, w_bf16):
    n = x2d.shape[0]
    return pl.pallas_call(
        _norm_proj_kernel,
        grid=(n // TM_PROJ,),
        in_specs=[
            pl.BlockSpec((TM_PROJ, D_MODEL), lambda i: (i, 0)),
            pl.BlockSpec((1, D_MODEL), lambda i: (0, 0)),
            pl.BlockSpec((D_MODEL, D_IN), lambda i: (0, 0)),
        ],
        out_specs=pl.BlockSpec((TM_PROJ, D_IN), lambda i: (i, 0)),
        out_shape=jax.ShapeDtypeStruct((n, D_IN), BF16),
        compiler_params=_cparams(("parallel",)),
        name="norm_proj",
    )(x2d, g, w_bf16)


def _retention_kernel(chunk_decay, q_ref, k_ref, v_ref, g_ref, cos_ref, sin_ref, dmask_ref,
                      qdec_ref, kdec_ref, gn_ref, o_ref, state_ref):
    n = pl.program_id(1)

    @pl.when(n == 0)
    def _():
        state_ref[...] = jnp.zeros_like(state_ref)

    cos = cos_ref[...]
    sin = sin_ref[...]

    def rot(t):
        t1, t2 = t[:, :128], t[:, 128:]
        return jnp.concatenate([t1 * cos - t2 * sin, t2 * cos + t1 * sin], axis=1)

    qr = rot(q_ref[0].astype(F32))
    kr = rot(k_ref[0].astype(F32)) * (RET_DK ** -0.5)
    qb = qr.astype(BF16)
    qd = (qr * qdec_ref[...]).astype(BF16)
    kd = kr * kdec_ref[...]
    lane = lax.broadcasted_iota(jnp.int32, (1, QW), 1)
    for h in range(HEADS):
        hm = jnp.where(((lane % 128) // 32) == h, 1.0, 0.0).astype(F32)
        km = (kr * hm).astype(BF16)
        vh = v_ref[0, :, h * RET_DV:(h + 1) * RET_DV]
        s = _dot_nt(qb, km) * dmask_ref[h]
        o = _dot(s.astype(BF16), vh)
        o = o + _dot(qd, state_ref[h].astype(BF16))
        upd = _dot_tn((kd * hm).astype(BF16), vh)
        state_ref[h] = state_ref[h] * chunk_decay[h] + upd
        ms = jnp.mean(o * o, axis=-1, keepdims=True)
        ro = o * lax.rsqrt(ms + EPS) * gn_ref[h:h + 1, :]
        gate = g_ref[0, :, h * RET_DV:(h + 1) * RET_DV].astype(F32)
        o_ref[0, :, h * RET_DV:(h + 1) * RET_DV] = (gate * jax.nn.sigmoid(gate) * ro).astype(BF16)


def _retention(proj3, cos_t, sin_t, dmask, qdec, kdec, gn, chunk_decay):
    b, s, _ = proj3.shape
    nc = s // CHUNK
    return pl.pallas_call(
        functools.partial(_retention_kernel, chunk_decay),
        grid=(b, nc),
        in_specs=[
            pl.BlockSpec((1, CHUNK, QW), lambda bi, n: (bi, n, COL_RQ)),
            pl.BlockSpec((1, CHUNK, QW), lambda bi, n: (bi, n, COL_RK)),
            pl.BlockSpec((1, CHUNK, 512), lambda bi, n: (bi, n, COL_RV512)),
            pl.BlockSpec((1, CHUNK, 512), lambda bi, n: (bi, n, COL_RG512)),
            pl.BlockSpec((CHUNK, 128), lambda bi, n: (n, 0)),
            pl.BlockSpec((CHUNK, 128), lambda bi, n: (n, 0)),
            pl.BlockSpec((HEADS, CHUNK, CHUNK), lambda bi, n: (0, 0, 0)),
            pl.BlockSpec((CHUNK, QW), lambda bi, n: (0, 0)),
            pl.BlockSpec((CHUNK, QW), lambda bi, n: (0, 0)),
            pl.BlockSpec((HEADS, RET_DV), lambda bi, n: (0, 0)),
        ],
        out_specs=pl.BlockSpec((1, CHUNK, HEADS * RET_DV), lambda bi, n: (bi, n, 0)),
        out_shape=jax.ShapeDtypeStruct((b, s, HEADS * RET_DV), BF16),
        scratch_shapes=[pltpu.VMEM((HEADS, QW, RET_DV), F32)],
        compiler_params=_cparams(("parallel", "arbitrary")),
        name="retention",
    )(proj3, proj3, proj3, proj3, cos_t, sin_t, dmask, qdec, kdec, gn)


def _moba_kernel(q_ref, k_ref, v_ref, gq_ref, gk_ref, bmat_ref, o_ref,
                 khat_ref, kmean_ref, qm_ref, sel_ref):
    i = pl.program_id(1)
    nb = khat_ref.shape[0] // CHUNK
    bmat = bmat_ref[...]
    lane = lax.broadcasted_iota(jnp.int32, (1, QW), 1)
    hmasks = [jnp.where((lane // DH) == h, 1.0, 0.0).astype(F32) for h in range(HEADS)]

    @pl.when(i == 0)
    def _():
        for n in range(nb):
            kb = k_ref[0, n * CHUNK:(n + 1) * CHUNK, :].astype(F32)
            kh = kb * lax.rsqrt(_head_mean_sq(kb, bmat) + EPS) * gk_ref[...]
            khat_ref[n * CHUNK:(n + 1) * CHUNK, :] = kh.astype(BF16)
            km = jnp.mean(kh, axis=0, keepdims=True)
            for h in range(HEADS):
                kmean_ref[h * nb + n:h * nb + n + 1, :] = km * hmasks[h]

    q = q_ref[0].astype(F32)
    qn = q * lax.rsqrt(_head_mean_sq(q, bmat) + EPS) * gq_ref[...]
    for h in range(HEADS):
        qm_ref[h] = (qn * (hmasks[h] * (DH ** -0.5))).astype(BF16)

    gate = _dot_nt(kmean_ref[...], qn, precision=lax.Precision.HIGHEST)
    n_iota = lax.broadcasted_iota(jnp.int32, (nb, CHUNK), 0)
    valid = n_iota < i
    sel_rows = []
    for h in range(HEADS):
        gm = jnp.where(valid, gate[h * nb:(h + 1) * nb], NEG_INF)
        cnt = jnp.zeros((nb, CHUNK), F32)
        for n in range(nb):
            gn = gm[n:n + 1, :]
            beats = jnp.where(gn > gm, 1.0, jnp.where(gn == gm, jnp.where(n < n_iota, 1.0, 0.0), 0.0))
            cnt = cnt + beats
        sel_rows.append(jnp.where(valid, jnp.where(cnt < MOBA_TOPK, 1.0, 0.0), 0.0))
    sel_t = jnp.concatenate(sel_rows + [jnp.zeros((128 - HEADS * nb, CHUNK), F32)], axis=0)
    sel_ref[...] = sel_t.T

    row = lax.broadcasted_iota(jnp.int32, (CHUNK, CHUNK), 0)
    col = lax.broadcasted_iota(jnp.int32, (CHUNK, CHUNK), 1)
    causal = col <= row

    for c in range(nb):
        @pl.when(i == c)
        def _(c=c):
            nk = (c + 1) * CHUNK
            kall = khat_ref[0:nk, :]
            vall = v_ref[0, 0:nk, :]
            out = jnp.zeros((CHUNK, QW), F32)
            for h in range(HEADS):
                s = _dot_nt(qm_ref[h], kall)
                parts = []
                for j in range(c):
                    sc = h * nb + j
                    parts.append(jnp.where(sel_ref[:, sc:sc + 1] > 0.5, s[:, j * CHUNK:(j + 1) * CHUNK], NEG_INF))
                parts.append(jnp.where(causal, s[:, c * CHUNK:], NEG_INF))
                m = jnp.max(functools.reduce(jnp.maximum, parts), axis=-1, keepdims=True)
                ps = [jnp.exp(p - m) for p in parts]
                l = jnp.sum(functools.reduce(jnp.add, ps), axis=-1, keepdims=True)
                p = jnp.concatenate(ps, axis=1).astype(BF16) if c else ps[0].astype(BF16)
                out = out + hmasks[h] * (_dot(p, vall) / l)
            o_ref[0] = out.astype(BF16)


def _moba(proj3, gq, gk, bmat):
    b, s, _ = proj3.shape
    nb = s // CHUNK
    return pl.pallas_call(
        _moba_kernel,
        grid=(b, nb),
        in_specs=[
            pl.BlockSpec((1, CHUNK, QW), lambda bi, i: (bi, i, COL_MQ)),
            pl.BlockSpec((1, s, QW), lambda bi, i: (bi, 0, COL_MK)),
            pl.BlockSpec((1, s, QW), lambda bi, i: (bi, 0, COL_MV)),
            pl.BlockSpec((1, QW), lambda bi, i: (0, 0)),
            pl.BlockSpec((1, QW), lambda bi, i: (0, 0)),
            pl.BlockSpec((QW, QW), lambda bi, i: (0, 0)),
        ],
        out_specs=pl.BlockSpec((1, CHUNK, QW), lambda bi, i: (bi, i, 0)),
        out_shape=jax.ShapeDtypeStruct((b, s, QW), BF16),
        scratch_shapes=[
            pltpu.VMEM((s, QW), BF16),
            pltpu.VMEM((HEADS * nb, QW), F32),
            pltpu.VMEM((HEADS, CHUNK, QW), BF16),
            pltpu.VMEM((CHUNK, 128), F32),
        ],
        compiler_params=_cparams(("parallel", "arbitrary")),
        name="moba",
    )(proj3, proj3, proj3, gq, gk, bmat)


def _mem_attn_kernel(q_ref, mem_ref, gm_ref, w_ref, gq_ref, gk_ref, bmat_ref, o_ref, ck_ref, cv_ref):
    i = pl.program_id(1)
    bmat = bmat_ref[...]

    @pl.when(i == 0)
    def _():
        mem = mem_ref[0]
        ms = jnp.mean(mem * mem, axis=-1, keepdims=True)
        mn = (mem * lax.rsqrt(ms + EPS) * gm_ref[...]).astype(BF16)
        kv = _dot(mn, w_ref[...])
        ck = kv[:, :QW]
        ck_ref[...] = (ck * lax.rsqrt(_head_mean_sq(ck, bmat) + EPS) * gk_ref[...]).astype(BF16)
        cv_ref[...] = kv[:, QW:].astype(BF16)

    q = q_ref[0].astype(F32)
    qn = q * lax.rsqrt(_head_mean_sq(q, bmat) + EPS) * gq_ref[...]
    lane = lax.broadcasted_iota(jnp.int32, (1, QW), 1)
    out = jnp.zeros(qn.shape, F32)
    for h in range(HEADS):
        hm = jnp.where((lane // DH) == h, 1.0, 0.0).astype(F32)
        s = _dot_nt((qn * (hm * (DH ** -0.5))).astype(BF16), ck_ref[...])
        p = jnp.exp(s - jnp.max(s, axis=-1, keepdims=True))
        l = jnp.sum(p, axis=-1, keepdims=True)
        out = out + hm * (_dot(p.astype(BF16), cv_ref[...]) / l)
    o_ref[0] = out.astype(BF16)


def _mem_attn(proj3, mem, gm, w_bf16, gq, gk, bmat):
    b, s, _ = proj3.shape
    return pl.pallas_call(
        _mem_attn_kernel,
        grid=(b, s // TQ_MEM),
        in_specs=[
            pl.BlockSpec((1, TQ_MEM, QW), lambda bi, i: (bi, i, COL_CQ)),
            pl.BlockSpec((1, MEM_LEN, D_MODEL), lambda bi, i: (bi, 0, 0)),
            pl.BlockSpec((1, D_MODEL), lambda bi, i: (0, 0)),
            pl.BlockSpec((D_MODEL, 2 * QW), lambda bi, i: (0, 0)),
            pl.BlockSpec((1, QW), lambda bi, i: (0, 0)),
            pl.BlockSpec((1, QW), lambda bi, i: (0, 0)),
            pl.BlockSpec((QW, QW), lambda bi, i: (0, 0)),
        ],
        out_specs=pl.BlockSpec((1, TQ_MEM, QW), lambda bi, i: (bi, i, 0)),
        out_shape=jax.ShapeDtypeStruct((b, s, QW), BF16),
        scratch_shapes=[pltpu.VMEM((MEM_LEN, QW), BF16), pltpu.VMEM((MEM_LEN, QW), BF16)],
        compiler_params=_cparams(("parallel", "arbitrary")),
        name="mem_attn",
    )(proj3, mem, gm, w_bf16, gq, gk, bmat)


ROUTER_ROWS = 32
EXP_ROW0 = 8


def _out_route_kernel(x_ref, yr_ref, ym_ref, yc_ref, wo_ref, g2_ref, wr_ref, br_ref, xe_ref, rt_ref):
    x1 = x_ref[...]
    x1 = x1 + _dot(yr_ref[...], wo_ref[0:512, :])
    x1 = x1 + _dot(ym_ref[...], wo_ref[512:768, :])
    x1 = x1 + _dot(yc_ref[...], wo_ref[768:1024, :])
    ms = jnp.mean(x1 * x1, axis=-1, keepdims=True)
    xn = x1 * lax.rsqrt(ms + EPS) * g2_ref[...]
    lt = _dot_nt(wr_ref[...], xn, precision=lax.Precision.HIGHEST) + br_ref[...]
    tm = lt.shape[1]
    gl = lt[0:N_GROUPS]
    gmax = jnp.max(gl, axis=0, keepdims=True)
    g_iota = lax.broadcasted_iota(jnp.int32, (N_GROUPS, tm), 0)
    gidx = jnp.min(jnp.where(gl == gmax, g_iota, N_GROUPS), axis=0, keepdims=True)
    g_w = 1.0 / jnp.sum(jnp.exp(gl - gmax), axis=0, keepdims=True)
    e_in = jnp.zeros((EXPERTS_PER_GROUP, tm), F32)
    for g in range(N_GROUPS):
        r0 = EXP_ROW0 + g * EXPERTS_PER_GROUP
        e_in = jnp.where(gidx == g, lt[r0:r0 + EXPERTS_PER_GROUP], e_in)
    e_iota = lax.broadcasted_iota(jnp.int32, (EXPERTS_PER_GROUP, tm), 0)
    cnt = jnp.zeros((EXPERTS_PER_GROUP, tm), F32)
    for n in range(EXPERTS_PER_GROUP):
        en = e_in[n:n + 1, :]
        cnt = cnt + jnp.where(en > e_in, 1.0, jnp.where(en == e_in, jnp.where(n < e_iota, 1.0, 0.0), 0.0))
    sel = cnt < 2.0
    a_idx = jnp.min(jnp.where(sel, e_iota, EXPERTS_PER_GROUP), axis=0, keepdims=True)
    b_idx = jnp.max(jnp.where(sel, e_iota, -1), axis=0, keepdims=True)
    la = jnp.sum(jnp.where(e_iota == a_idx, e_in, 0.0), axis=0, keepdims=True)
    lb = jnp.sum(jnp.where(e_iota == b_idx, e_in, 0.0), axis=0, keepdims=True)
    mx = jnp.maximum(la, lb)
    ea = jnp.exp(la - mx)
    eb = jnp.exp(lb - mx)
    wa = g_w * (ea / (ea + eb))
    wb = g_w * (eb / (ea + eb))
    pair_off = jnp.where(a_idx == 0, 0, jnp.where(a_idx == 1, 3, 5))
    bucket = (gidx * len(PAIRS) + pair_off + (b_idx - a_idx - 1)).astype(F32)
    rows = jnp.concatenate([bucket, wa, wb, jnp.zeros((5, tm), F32)], axis=0)
    rt_ref[...] = rows
    ext = jnp.concatenate([rows, jnp.zeros((EXT - 8, tm), F32)], axis=0).T
    xe_ref[:, :D_MODEL] = x1
    xe_ref[:, D_MODEL:] = ext


def _out_route(x2d, y_ret, y_moba, y_mem, wo_bf16, g2, wr_t, br):
    n = x2d.shape[0]
    return pl.pallas_call(
        _out_route_kernel,
        grid=(n // TM_OUT,),
        in_specs=[
            pl.BlockSpec((TM_OUT, D_MODEL), lambda i: (i, 0)),
            pl.BlockSpec((TM_OUT, 512), lambda i: (i, 0)),
            pl.BlockSpec((TM_OUT, QW), lambda i: (i, 0)),
            pl.BlockSpec((TM_OUT, QW), lambda i: (i, 0)),
            pl.BlockSpec((D_MODEL, D_MODEL), lambda i: (0, 0)),
            pl.BlockSpec((1, D_MODEL), lambda i: (0, 0)),
            pl.BlockSpec((ROUTER_ROWS, D_MODEL), lambda i: (0, 0)),
            pl.BlockSpec((ROUTER_ROWS, 1), lambda i: (0, 0)),
        ],
        out_specs=[
            pl.BlockSpec((TM_OUT, ROW_W), lambda i: (i, 0)),
            pl.BlockSpec((8, TM_OUT), lambda i: (0, i)),
        ],
        out_shape=[jax.ShapeDtypeStruct((n, ROW_W), F32), jax.ShapeDtypeStruct((8, n), F32)],
        compiler_params=_cparams(("parallel",)),
        name="out_route",
    )(x2d, y_ret, y_moba, y_mem, wo_bf16, g2, wr_t, br)


def _moe_kernel(tile_ref, ea_ref, eb_ref, lo_ref, hi_ref, chg_ref,
                tokp_ref, tokc_ref, tokn_ref, x_hbm, g2_ref,
                wga_ref, wua_ref, wda_ref, wgb_ref, wub_ref, wdb_ref, o_hbm,
                xbuf, obuf, xn_ref, bga, bua, bda, bgb, bub, bdb, gsem, ssem):
    s = pl.program_id(0)
    n_steps = pl.num_programs(0)
    n_tiles = o_hbm.shape[0] // TM_MOE
    tile = tile_ref[s]
    lo = lo_ref[s]
    hi = hi_ref[s]
    slot = tile % 2
    valid = hi > lo
    first = jnp.logical_and(valid, lo == 0)

    def gather_start(tok_ref, sl):
        for r in range(TM_MOE):
            pltpu.make_async_copy(x_hbm.at[pl.ds(tok_ref[0, 0, r], 1)], xbuf.at[sl, pl.ds(r, 1)],
                                  gsem.at[sl]).start()

    def gather_wait(sl):
        pltpu.make_async_copy(x_hbm.at[pl.ds(0, TM_MOE)], xbuf.at[sl], gsem.at[sl]).wait()

    def scatter_start(tok_ref, sl):
        for r in range(TM_MOE):
            pltpu.make_async_copy(obuf.at[sl, pl.ds(r, 1)], o_hbm.at[pl.ds(tok_ref[0, 0, r], 1)],
                                  ssem.at[sl]).start()

    def scatter_wait(sl):
        pltpu.make_async_copy(obuf.at[sl], o_hbm.at[pl.ds(0, TM_MOE)], ssem.at[sl]).wait()

    @pl.when(s == 0)
    def _():
        gather_start(tokc_ref, 0)

    @pl.when(chg_ref[s] == 1)
    def _():
        bga[...] = wga_ref[0].astype(BF16)
        bua[...] = wua_ref[0].astype(BF16)
        bda[...] = wda_ref[0].astype(BF16)
        bgb[...] = wgb_ref[0].astype(BF16)
        bub[...] = wub_ref[0].astype(BF16)
        bdb[...] = wdb_ref[0].astype(BF16)

    def segment_out():
        xn = xn_ref[...]

        def expert(wg, wu, wd):
            g = _dot(xn, wg[...])
            u = _dot(xn, wu[...])
            return _dot((g * jax.nn.sigmoid(g) * u).astype(BF16), wd[...])

        wa = xbuf[slot, :, D_MODEL + 1:D_MODEL + 2]
        wb = xbuf[slot, :, D_MODEL + 2:D_MODEL + 3]
        y = wa * expert(bga, bua, bda) + wb * expert(bgb, bub, bdb)
        row = lax.broadcasted_iota(jnp.int32, (TM_MOE, 1), 0)
        return jnp.where(jnp.logical_and(row >= lo, row < hi), y, 0.0)

    def first_segment(next_gather, prev_scatter):
        if next_gather:
            gather_start(tokn_ref, 1 - slot)
        if prev_scatter:
            scatter_start(tokp_ref, 1 - slot)
        x1 = xbuf[slot, :, :D_MODEL]
        ms = jnp.mean(x1 * x1, axis=-1, keepdims=True)
        xn_ref[...] = (x1 * lax.rsqrt(ms + EPS) * g2_ref[...]).astype(BF16)
        obuf[slot] = x1 + segment_out()

    @pl.when(first)
    def _():
        @pl.when(tile >= 2)
        def _():
            scatter_wait(slot)

        gather_wait(slot)

    @pl.when(jnp.logical_and(first, tile == 0))
    def _():
        first_segment(True, False)

    @pl.when(jnp.logical_and(first, jnp.logical_and(tile > 0, tile < n_tiles - 1)))
    def _():
        first_segment(True, True)

    @pl.when(jnp.logical_and(first, tile == n_tiles - 1))
    def _():
        first_segment(False, True)

    @pl.when(jnp.logical_and(valid, lo > 0))
    def _():
        obuf[slot] = obuf[slot] + segment_out()

    @pl.when(s == n_steps - 1)
    def _():
        scatter_start(tokc_ref, (n_tiles - 1) % 2)
        scatter_wait(0)
        scatter_wait(1)


def _moe(x_ext, order, steps, g2, wg, wu, wd):
    n = x_ext.shape[0]
    n_tiles = n // TM_MOE
    st_tile, st_ea, st_eb, st_lo, st_hi, st_chg = steps
    tok = order.reshape(n_tiles, 1, TM_MOE)

    def tok_spec(shift):
        return pl.BlockSpec((1, 1, TM_MOE),
                            lambda s, tile, *_: (jnp.clip(tile[s] + shift, 0, n_tiles - 1), 0, 0),
                            memory_space=pltpu.SMEM)

    wspec_a = lambda shape: pl.BlockSpec(shape, lambda s, tile, ea, eb, *_: (ea[s], 0, 0))
    wspec_b = lambda shape: pl.BlockSpec(shape, lambda s, tile, ea, eb, *_: (eb[s], 0, 0))
    gu, dn = (D_MODEL, EXPERT_FF), (EXPERT_FF, D_MODEL)
    grid_spec = pltpu.PrefetchScalarGridSpec(
        num_scalar_prefetch=6,
        grid=(st_tile.shape[0],),
        in_specs=[
            tok_spec(-1), tok_spec(0), tok_spec(1),
            pl.BlockSpec(memory_space=pl.ANY),
            pl.BlockSpec((1, D_MODEL), lambda s, *_: (0, 0)),
            wspec_a((1,) + gu), wspec_a((1,) + gu), wspec_a((1,) + dn),
            wspec_b((1,) + gu), wspec_b((1,) + gu), wspec_b((1,) + dn),
        ],
        out_specs=pl.BlockSpec(memory_space=pl.ANY),
        scratch_shapes=[
            pltpu.VMEM((2, TM_MOE, ROW_W), F32),
            pltpu.VMEM((2, TM_MOE, D_MODEL), F32),
            pltpu.VMEM((TM_MOE, D_MODEL), BF16),
            pltpu.VMEM(gu, BF16), pltpu.VMEM(gu, BF16), pltpu.VMEM(dn, BF16),
            pltpu.VMEM(gu, BF16), pltpu.VMEM(gu, BF16), pltpu.VMEM(dn, BF16),
            pltpu.SemaphoreType.DMA((2,)),
            pltpu.SemaphoreType.DMA((2,)),
        ],
    )
    return pl.pallas_call(
        _moe_kernel,
        grid_spec=grid_spec,
        out_shape=jax.ShapeDtypeStruct((n, D_MODEL), F32),
        compiler_params=_cparams(("arbitrary",)),
        name="moe",
    )(st_tile, st_ea, st_eb, st_lo, st_hi, st_chg, tok, tok, tok, x_ext, g2, wg, wu, wd, wg, wu, wd)


def _moe_schedule(bucket, n, expert_base):
    i32 = jnp.int32
    n_tiles = n // TM_MOE
    onehot = (bucket[:, None] == jnp.arange(N_BUCKETS, dtype=i32)[None, :]).astype(i32)
    counts = jnp.sum(onehot, axis=0)
    cend = jnp.cumsum(counts)
    cstart = cend - counts
    order = lax.top_k(-(bucket * n + jnp.arange(n, dtype=i32)), n)[1].astype(i32)

    v = jnp.concatenate([jnp.arange(n_tiles, dtype=i32) * TM_MOE, cstart])
    idx = jnp.arange(v.shape[0], dtype=i32)
    before = (v[None, :] < v[:, None]) | ((v[None, :] == v[:, None]) & (idx[None, :] < idx[:, None]))
    rank = jnp.sum(before.astype(i32), axis=1)
    starts = jnp.sum(jnp.where(rank[:, None] == idx[None, :], v[:, None], 0), axis=0)
    ends = jnp.concatenate([starts[1:], jnp.array([n], i32)])
    tile = jnp.minimum(starts // TM_MOE, n_tiles - 1)
    sb = jnp.minimum(jnp.sum((cend[None, :] <= starts[:, None]).astype(i32), axis=1), N_BUCKETS - 1)
    lo = starts - tile * TM_MOE
    hi = ends - tile * TM_MOE
    onehot_sb = (sb[:, None] == jnp.arange(N_BUCKETS, dtype=i32)[None, :]).astype(i32)
    chg = ((hi > lo) & (starts == jnp.sum(onehot_sb * cstart[None, :], axis=1))).astype(i32)
    pa = jnp.array([p[0] for p in PAIRS], i32)
    pb = jnp.array([p[1] for p in PAIRS], i32)
    grp = sb // len(PAIRS)
    ea = expert_base + grp * EXPERTS_PER_GROUP + pa[sb % len(PAIRS)]
    eb = expert_base + grp * EXPERTS_PER_GROUP + pb[sb % len(PAIRS)]
    return order, (tile.astype(i32), ea.astype(i32), eb.astype(i32), lo.astype(i32), hi.astype(i32), chg)


def _retention_constants(s):
    half = RET_DK // 2
    inv = 1.0 / (10000.0 ** jnp.linspace(0.0, 1.0, half, dtype=F32))
    ang = jnp.arange(s).astype(F32)[:, None] * inv[None, :]
    cos_t = jnp.tile(jnp.cos(ang), (1, HEADS))
    sin_t = jnp.tile(jnp.sin(ang), (1, HEADS))
    log_g = jnp.log1p(-(2.0 ** (-5.0 - jnp.arange(HEADS, dtype=F32))))
    nn = jnp.arange(CHUNK, dtype=F32)
    diff = nn[:, None] - nn[None, :]
    dmask = jnp.where(diff >= 0, jnp.exp(log_g[:, None, None] * jnp.maximum(diff, 0.0)), 0.0)
    q_decay = jnp.exp(log_g[:, None] * (nn + 1.0))
    k_decay = jnp.exp(log_g[:, None] * (CHUNK - 1.0 - nn))
    lane_head = (jnp.arange(QW) % 128) // 32
    qdec = q_decay.T[:, lane_head]
    kdec = k_decay.T[:, lane_head]
    chunk_decay = tuple(float(np.exp(np.log1p(-(2.0 ** (-5.0 - h))) * CHUNK)) for h in range(HEADS))
    return cos_t, sin_t, dmask, qdec, kdec, chunk_decay


def _proj_column_order():
    perm = np.arange(D_IN)
    c = np.arange(QW)
    src = ((c % 128) // 32) * RET_DK + (c // 128) * (RET_DK // 2) + (c % 32)
    perm[0:QW] = src
    perm[QW:2 * QW] = QW + src
    return perm


def kernel(x, mem, norm1_g, w_in, ret_norm_g, moba_qn_g, moba_kn_g, mem_norm_g, w_mem_kv, mem_qn_g, mem_kn_g,
           w_out, norm2_g, w_router_grp, b_router_grp, w_router_exp, b_router_exp, w_exp_gate, w_exp_up,
           w_exp_down):
    b, s, d = x.shape
    assert d == D_MODEL and s % CHUNK == 0 and (b * s) % TM_PROJ == 0 and s % TQ_MEM == 0
    n = b * s
    depth = w_in.shape[0]
    cos_t, sin_t, dmask, qdec, kdec, chunk_decay = _retention_constants(s)
    perm = _proj_column_order()
    head_of = np.arange(QW) // DH
    bmat = jnp.asarray((head_of[:, None] == head_of[None, :]).astype(np.float32) / DH, BF16)

    wg_all = w_exp_gate.reshape(depth * N_EXPERTS, D_MODEL, EXPERT_FF)
    wu_all = w_exp_up.reshape(depth * N_EXPERTS, D_MODEL, EXPERT_FF)
    wd_all = w_exp_down.reshape(depth * N_EXPERTS, EXPERT_FF, D_MODEL)

    x2d = x.reshape(n, d)
    for l in range(depth):
        proj = _norm_proj(x2d, norm1_g[l][None, :], w_in[l][:, perm].astype(BF16))
        proj3 = proj.reshape(b, s, D_IN)
        y_ret = _retention(proj3, cos_t, sin_t, dmask, qdec, kdec, ret_norm_g[l], chunk_decay)
        y_moba = _moba(proj3, jnp.tile(moba_qn_g[l], HEADS)[None, :], jnp.tile(moba_kn_g[l], HEADS)[None, :], bmat)
        y_mem = _mem_attn(proj3, mem, mem_norm_g[l][None, :], w_mem_kv[l].astype(BF16),
                          jnp.tile(mem_qn_g[l], HEADS)[None, :], jnp.tile(mem_kn_g[l], HEADS)[None, :], bmat)

        wr_t = jnp.zeros((ROUTER_ROWS, d), F32)
        wr_t = wr_t.at[0:N_GROUPS].set(w_router_grp[l].T).at[EXP_ROW0:EXP_ROW0 + N_EXPERTS].set(w_router_exp[l].T)
        br = jnp.zeros((ROUTER_ROWS, 1), F32)
        br = br.at[0:N_GROUPS, 0].set(b_router_grp[l]).at[EXP_ROW0:EXP_ROW0 + N_EXPERTS, 0].set(b_router_exp[l])
        x_ext, route = _out_route(x2d, y_ret.reshape(n, -1), y_moba.reshape(n, -1), y_mem.reshape(n, -1),
                                  w_out[l].astype(BF16), norm2_g[l][None, :], wr_t, br)

        order, steps = _moe_schedule(route[0].astype(jnp.int32), n, l * N_EXPERTS)
        x2d = _moe(x_ext, order, steps, norm2_g[l][None, :], wg_all, wu_all, wd_all)
    return x2d.reshape(b, s, d)
```
